```python
import math
import jax, jax.numpy as jnp
from jax import lax
import numpy as np

D_MODEL = 2048
BATCH = 8
SEQ = 4096
DEPTH = 2

N_EVEN = (DEPTH + 1) // 2
N_ODD = DEPTH // 2

H_A = 8
DA = 64
W_A_QK = H_A * 2 * DA
W_A_V = H_A * 2 * DA
Q_BLOCK = 128

H_B = 8
DK_B = 128
DV_B = 128
W_B = H_B * DK_B
CHUNK = 64

AB_SIZES = (W_A_QK, W_A_QK, W_A_V, W_B, H_B * DV_B, H_B * DV_B, W_B, W_B)
AB_IN = sum(AB_SIZES)
AB_MIX = W_A_V + H_B * DV_B

CONV_W = 3

D_FF = 4 * D_MODEL

EPS = 1e-6

kernel_name = "hybrid_diffattn_hgrn2_shortconv_encoder"


def rms_norm(x, g):
    xf = x.astype(jnp.float32)
    y = xf * lax.rsqrt(jnp.mean(xf * xf, axis=-1, keepdims=True) + EPS)
    return (y * g.astype(jnp.float32)).astype(x.dtype)


def alibi_slopes(n_heads):
    return 2.0 ** (-8.0 * jnp.arange(1, n_heads + 1, dtype=jnp.float32) / n_heads)


def diff_attention(q, k, v, lam, slopes):
    B, H, _, T, d = q.shape
    nb = T // Q_BLOCK
    q_blocks = jnp.moveaxis(q.reshape(B, H, 2, nb, Q_BLOCK, d), 3, 0)
    starts = jnp.arange(nb, dtype=jnp.int32) * Q_BLOCK
    k_pos = jnp.arange(T, dtype=jnp.int32)
    scale = d ** -0.5

    def block(args):
        q_blk, start = args
        s = jnp.einsum('bhmqd,bhmkd->bhmqk', q_blk, k).astype(jnp.float32) * scale
        q_pos = start + jnp.arange(Q_BLOCK, dtype=jnp.int32)
        dist = jnp.abs(q_pos[:, None] - k_pos[None, :]).astype(jnp.float32)
        s = s - slopes[None, :, None, None, None] * dist
        p = jax.nn.softmax(s, axis=-1)
        a = p[:, :, 0] - lam.astype(jnp.float32) * p[:, :, 1]
        return jnp.einsum('bhqk,bhkv->bhqv', a.astype(v.dtype), v)

    o = lax.map(block, (q_blocks, starts))
    return jnp.moveaxis(o, 0, 2).reshape(B, H, T, v.shape[-1])


def hgrn2_scan(q, k, v, log_f):
    B, H, T, dk = q.shape
    dv = v.shape[-1]
    n = T // CHUNK

    def to_chunks(a):
        return jnp.moveaxis(a.reshape(B, H, n, CHUNK, a.shape[-1]), 2, 0)

    mask = jnp.tril(jnp.ones((CHUNK, CHUNK), dtype=bool))[:, :, None]

    def step(S, inp):
        qc, kc, vc, gc = inp
        b = jnp.cumsum(gc, axis=2)
        inter = jnp.einsum('bhtk,bhkv->bhtv', qc * jnp.exp(b), S)
        rel = b[:, :, :, None, :] - b[:, :, None, :, :]
        decay = jnp.where(mask, jnp.exp(jnp.where(mask, rel, 0.0)), 0.0)
        att = jnp.einsum('bhtk,bhtsk,bhsk->bhts', qc, decay, kc)
        intra = jnp.einsum('bhts,bhsv->bhtv', att, vc)
        b_last = b[:, :, -1:, :]
        S_new = jnp.exp(b_last[:, :, 0, :, None]) * S + jnp.einsum(
            'bhsk,bhsv->bhkv', kc * jnp.exp(b_last - b), vc)
        return S_new, inter + intra

    S0 = jnp.zeros((B, H, dk, dv), jnp.float32)
    _, o = lax.scan(step, S0, (to_chunks(q), to_chunks(k), to_chunks(v), to_chunks(log_f)))
    return jnp.moveaxis(o, 0, 2).reshape(B, H, T, dv)


def diff_hgrn_mixer(xn, w_in, w_out, lam_p, subln_g, lb_table, hgrn_g, layer):
    B, T, _ = xn.shape
    proj = xn @ w_in
    idx = np.cumsum(AB_SIZES)[:-1].tolist()
    q, k, v, hq, hi, hg, hf_fwd, hf_bwd = jnp.split(proj, idx, axis=-1)

    qa = q.reshape(B, T, H_A, 2, DA).transpose(0, 2, 3, 1, 4)
    ka = k.reshape(B, T, H_A, 2, DA).transpose(0, 2, 3, 1, 4)
    va = v.reshape(B, T, H_A, 2 * DA).transpose(0, 2, 1, 3)
    lam_init = 0.8 - 0.6 * math.exp(-0.3 * layer)
    lp = lam_p.astype(jnp.float32)
    lam = jnp.exp(jnp.sum(lp[0] * lp[1])) - jnp.exp(jnp.sum(lp[2] * lp[3])) + lam_init
    oa = diff_attention(qa, ka, va, lam, alibi_slopes(H_A))
    oa = rms_norm(oa, subln_g) * (1.0 - lam_init)
    oa = oa.transpose(0, 2, 1, 3).reshape(B, T, W_A_V)

    lb = jnp.cumsum(jax.nn.softmax(lb_table.astype(jnp.float32), axis=1), axis=1)[:, layer]

    def heads(a):
        return a.astype(jnp.float32).reshape(B, T, H_B, -1).transpose(0, 2, 1, 3)

    def gates(z, lb_d):
        zf = z.astype(jnp.float32)
        f = lb_d + (1.0 - lb_d) * jax.nn.sigmoid(zf)
        k_in = (1.0 - lb_d) * jax.nn.sigmoid(-zf)
        return heads(k_in), heads(jnp.log(f))

    qh, vh = heads(hq), heads(hi)
    k_f, g_f = gates(hf_fwd, lb[0])
    k_b, g_b = gates(hf_bwd, lb[1])
    flip = lambda a: jnp.flip(a, axis=2)
    ob = hgrn2_scan(qh, k_f, vh, g_f) + flip(hgrn2_scan(flip(qh), flip(k_b), flip(vh), flip(g_b)))
    ob = rms_norm(ob, hgrn_g.reshape(H_B, 1, DV_B))
    ob = ob.transpose(0, 2, 1, 3).reshape(B, T, H_B * DV_B).astype(xn.dtype)
    ob = ob * jax.nn.silu(hg)

    return jnp.concatenate([oa, ob], axis=-1) @ w_out


def short_conv_mixer(xn, w_in, conv_w, w_out):
    b_gate, c_gate, h = jnp.split(xn @ w_in, 3, axis=-1)
    u = c_gate * h
    conv = lax.conv_general_dilated(
        u, conv_w[:, None, :], window_strides=(1,), padding=[(1, 1)],
        dimension_numbers=('NWC', 'WIO', 'NWC'), feature_group_count=u.shape[-1])
    return (b_gate * conv) @ w_out


def sq_relu_mlp(xn, w_up, w_down):
    return jnp.square(jax.nn.relu(xn @ w_up)) @ w_down


def setup_inputs(seed: int = 0) -> dict:
    key = jax.random.key(seed)
    ks = jax.random.split(key, 16)
    nrm = lambda k, shape, scale: jax.random.normal(k, shape, jnp.float32) * scale
    return {
        "x": nrm(ks[0], (BATCH, SEQ, D_MODEL), 1.0),
        "mix_norm_g": 1.0 + nrm(ks[1], (DEPTH, D_MODEL), 0.02),
        "mlp_norm_g": 1.0 + nrm(ks[2], (DEPTH, D_MODEL), 0.02),
        "final_norm_g": 1.0 + nrm(ks[3], (D_MODEL,), 0.02),
        "ab_w_in": nrm(ks[4], (N_EVEN, D_MODEL, AB_IN), D_MODEL ** -0.5),
        "ab_w_out": nrm(ks[5], (N_EVEN, AB_MIX, D_MODEL), AB_MIX ** -0.5),
        "diff_lambda": nrm(ks[6], (N_EVEN, 4, DA), 0.1),
        "diff_subln_g": 1.0 + nrm(ks[7], (N_EVEN, 2 * DA), 0.02),
        "hgrn_lb": nrm(ks[8], (2, DEPTH + 1, W_B), 0.1),
        "hgrn_norm_g": 1.0 + nrm(ks[9], (N_EVEN, H_B * DV_B), 0.02),
        "conv_w_in": nrm(ks[10], (N_ODD, D_MODEL, 3 * D_MODEL), D_MODEL ** -0.5),
        "conv_w": nrm(ks[11], (N_ODD, CONV_W, D_MODEL), CONV_W ** -0.5),
        "conv_w_out": nrm(ks[12], (N_ODD, D_MODEL, D_MODEL), D_MODEL ** -0.5),
        "mlp_w_up": nrm(ks[13], (DEPTH, D_MODEL, D_FF), D_MODEL ** -0.5),
        "mlp_w_down": nrm(ks[14], (DEPTH, D_FF, D_MODEL), D_FF ** -0.5),
    }


def reference(x, mix_norm_g, mlp_norm_g, final_norm_g, ab_w_in, ab_w_out, diff_lambda,
              diff_subln_g, hgrn_lb, hgrn_norm_g, conv_w_in, conv_w, conv_w_out,
              mlp_w_up, mlp_w_down):
    h = x
    for l in range(DEPTH):
        j = l // 2
        hn = rms_norm(h, mix_norm_g[l])
        if l % 2 == 0:
            h = h + diff_hgrn_mixer(hn, ab_w_in[j], ab_w_out[j], diff_lambda[j],
                                    diff_subln_g[j], hgrn_lb, hgrn_norm_g[j], l)
        else:
            h = h + short_conv_mixer(hn, conv_w_in[j], conv_w[j], conv_w_out[j])
        hn = rms_norm(h, mlp_norm_g[l])
        h = h + sq_relu_mlp(hn, mlp_w_up[l], mlp_w_down[l])
    return rms_norm(h, final_norm_g)
```

```python
import functools
import math

import numpy as np
import jax
import jax.numpy as jnp
from jax import lax
from jax.experimental import pallas as pl
from jax.experimental.pallas import tpu as pltpu

EPS = 1e-6
N_HEADS = 8
HEAD_W = 128
DA = 64
HG_CHUNK = 64
CONV_HALO = 16
MXU_DTYPE = jnp.bfloat16
VMEM_LIMIT = 56 * 1024 * 1024


def _tile(dim, pref):
    return pref if dim % pref == 0 else dim


def _rms(x, g):
    ms = jnp.mean(x * x, axis=-1, keepdims=True)
    return x * lax.rsqrt(ms + EPS) * g


def _params(*sem):
    return pltpu.CompilerParams(dimension_semantics=sem, vmem_limit_bytes=VMEM_LIMIT)


def _norm_matmul_kernel(x_ref, g_ref, w_ref, o_ref, hn_ref):
    @pl.when(pl.program_id(1) == 0)
    def _():
        hn_ref[...] = _rms(x_ref[...], g_ref[...]).astype(hn_ref.dtype)

    o_ref[...] = jnp.dot(hn_ref[...], w_ref[...],
                         preferred_element_type=jnp.float32).astype(o_ref.dtype)


def _norm_matmul(x, g, w, out_dtype, tm=1024, tn=1024):
    m, k = x.shape
    n = w.shape[1]
    tm, tn = _tile(m, tm), _tile(n, tn)
    return pl.pallas_call(
        _norm_matmul_kernel,
        grid=(m // tm, n // tn),
        in_specs=[pl.BlockSpec((tm, k), lambda i, j: (i, 0)),
                  pl.BlockSpec((1, k), lambda i, j: (0, 0)),
                  pl.BlockSpec((k, tn), lambda i, j: (0, j))],
        out_specs=pl.BlockSpec((tm, tn), lambda i, j: (i, j)),
        out_shape=jax.ShapeDtypeStruct((m, n), out_dtype),
        scratch_shapes=[pltpu.VMEM((tm, k), MXU_DTYPE)],
        compiler_params=_params("parallel", "arbitrary"),
    )(x, g.reshape(1, k), w)


def _proj2_kernel(a_ref, b_ref, wa_ref, wb_ref, r_ref, o_ref):
    acc = jnp.dot(a_ref[...], wa_ref[...], preferred_element_type=jnp.float32)
    acc += jnp.dot(b_ref[...], wb_ref[...], preferred_element_type=jnp.float32)
    o_ref[...] = r_ref[...] + acc


def _proj2_residual(a, b, wa, wb, res, tm=1024, tn=1024):
    m, ka = a.shape
    kb = b.shape[1]
    n = wa.shape[1]
    tm, tn = _tile(m, tm), _tile(n, tn)
    return pl.pallas_call(
        _proj2_kernel,
        grid=(m // tm, n // tn),
        in_specs=[pl.BlockSpec((tm, ka), lambda i, j: (i, 0)),
                  pl.BlockSpec((tm, kb), lambda i, j: (i, 0)),
                  pl.BlockSpec((ka, tn), lambda i, j: (0, j)),
                  pl.BlockSpec((kb, tn), lambda i, j: (0, j)),
                  pl.BlockSpec((tm, tn), lambda i, j: (i, j))],
        out_specs=pl.BlockSpec((tm, tn), lambda i, j: (i, j)),
        out_shape=jax.ShapeDtypeStruct((m, n), jnp.float32),
        compiler_params=_params("parallel", "arbitrary"),
    )(a, b, wa, wb, res)


def _mlp_kernel(h_ref, g_ref, wu_ref, wd_ref, gf_ref, o_ref, hn_ref, *, final_norm):
    j = pl.program_id(1)

    @pl.when(j == 0)
    def _():
        h = h_ref[...]
        hn_ref[...] = _rms(h, g_ref[...]).astype(hn_ref.dtype)
        o_ref[...] = h

    a = jnp.dot(hn_ref[...], wu_ref[...], preferred_element_type=jnp.float32)
    a = jnp.maximum(a, 0.0)
    a = (a * a).astype(wd_ref.dtype)
    o_ref[...] += jnp.dot(a, wd_ref[...], preferred_element_type=jnp.float32)

    if final_norm:
        @pl.when(j == pl.num_programs(1) - 1)
        def _():
            o_ref[...] = _rms(o_ref[...], gf_ref[...])


def _mlp(h, g, w_up, w_down, g_final, final_norm, tm=1024, tf=512):
    m, d = h.shape
    f = w_up.shape[1]
    tm, tf = _tile(m, tm), _tile(f, tf)
    return pl.pallas_call(
        functools.partial(_mlp_kernel, final_norm=final_norm),
        grid=(m // tm, f // tf),
        in_specs=[pl.BlockSpec((tm, d), lambda i, j: (i, 0)),
                  pl.BlockSpec((1, d), lambda i, j: (0, 0)),
                  pl.BlockSpec((d, tf), lambda i, j: (0, j)),
                  pl.BlockSpec((tf, d), lambda i, j: (j, 0)),
                  pl.BlockSpec((1, d), lambda i, j: (0, 0))],
        out_specs=pl.BlockSpec((tm, d), lambda i, j: (i, 0)),
        out_shape=jax.ShapeDtypeStruct((m, d), jnp.float32),
        scratch_shapes=[pltpu.VMEM((tm, d), MXU_DTYPE)],
        compiler_params=_params("parallel", "arbitrary"),
    )(h, g.reshape(1, d), w_up, w_down, g_final.reshape(1, d))


def _attn_kernel(q_ref, k_ref, v_ref, slope_ref, lam_ref, g_ref, o_ref, *, tq, tk, lam_init):
    t = k_ref.shape[0]
    rows = 2 * tq
    q = q_ref[...]
    lane = lax.broadcasted_iota(jnp.int32, q.shape, 1)
    qs = q * jnp.asarray(DA ** -0.5, q.dtype)
    zero = jnp.zeros_like(qs)
    qq = jnp.concatenate([jnp.where(lane < DA, qs, zero),
                          jnp.where(lane >= DA, qs, zero)], axis=0)
    slope = slope_ref[...][:, :1]
    row = lax.broadcasted_iota(jnp.int32, (rows, tk), 0)
    col = lax.broadcasted_iota(jnp.int32, (rows, tk), 1)
    rel = (jnp.where(row >= tq, row - tq, row) - col).astype(jnp.float32)
    q0 = pl.program_id(2) * tq

    def body(j, carry):
        m, l, acc = carry
        k0 = pl.multiple_of(j * tk, tk)
        kj = k_ref[pl.ds(k0, tk), :]
        vj = v_ref[pl.ds(k0, tk), :]
        s = lax.dot_general(qq, kj, (((1,), (1,)), ((), ())),
                            preferred_element_type=jnp.float32)
        s = s - slope * jnp.abs(rel + (q0 - k0).astype(jnp.float32))
        m_new = jnp.maximum(m, jnp.max(s, axis=1, keepdims=True))
        alpha = jnp.exp(m - m_new)
        p = jnp.exp(s - m_new)
        l = alpha * l + jnp.sum(p, axis=1, keepdims=True)
        acc = alpha * acc + jnp.dot(p.astype(vj.dtype), vj,
                                    preferred_element_type=jnp.float32)
        return m_new, l, acc

    m0 = jnp.full((rows, 1), -jnp.inf, jnp.float32)
    l0 = jnp.zeros((rows, 1), jnp.float32)
    a0 = jnp.zeros((rows, HEAD_W), jnp.float32)
    _, l, acc = lax.fori_loop(0, t // tk, body, (m0, l0, a0))

    lp = lam_ref[...]
    lam = (jnp.exp(jnp.sum(lp[0:1] * lp[1:2], axis=1, keepdims=True))
           - jnp.exp(jnp.sum(lp[2:3] * lp[3:4], axis=1, keepdims=True)) + lam_init)
    o = acc / l
    o = o[:tq] - lam * o[tq:]
    o_ref[...] = (_rms(o, g_ref[...]) * (1.0 - lam_init)).astype(o_ref.dtype)


def _attention(proj, slopes, lam_p, subln_g, lam_init, tq=256, tk=512):
    b, t, _ = proj.shape
    tq, tk = _tile(t, tq), _tile(t, tk)
    h = N_HEADS
    return pl.pallas_call(
        functools.partial(_attn_kernel, tq=tq, tk=tk, lam_init=lam_init),
        grid=(b, h, t // tq),
        in_specs=[pl.BlockSpec((None, tq, HEAD_W), lambda bi, hi, qi: (bi, qi, hi)),
                  pl.BlockSpec((None, t, HEAD_W), lambda bi, hi, qi: (bi, 0, h + hi)),
                  pl.BlockSpec((None, t, HEAD_W), lambda bi, hi, qi: (bi, 0, 2 * h + hi)),
                  pl.BlockSpec((None, 1, HEAD_W), lambda bi, hi, qi: (hi, 0, 0)),
                  pl.BlockSpec(lam_p.shape, lambda bi, hi, qi: (0, 0)),
                  pl.BlockSpec((1, HEAD_W), lambda bi, hi, qi: (0, 0))],
        out_specs=pl.BlockSpec((None, tq, HEAD_W), lambda bi, hi, qi: (bi, qi, hi)),
        out_shape=jax.ShapeDtypeStruct((b, t, h * HEAD_W), MXU_DTYPE),
        compiler_params=_params("parallel", "parallel", "arbitrary"),
    )(proj, proj, proj, slopes, lam_p, subln_g.reshape(1, HEAD_W))


def _hgrn_constants(c):
    levels = []
    size = 2
    while size <= c:
        levels.append(size)
        size *= 2
    expo = np.zeros((len(levels) + 1, c, c), np.float32)
    mask = np.zeros((len(levels) + 1, c, c), np.float32)
    for li, size in enumerate(levels):
        half = size // 2
        for t in range(c):
            pos = t % size
            bd = t - pos + half - 1
            if pos >= half:
                expo[li, t, bd + 1:t + 1] = 1.0
                mask[li, t, t - pos:t - pos + half] = 1.0
            else:
                expo[li, t, t + 1:bd + 1] = 1.0
    expo[-1] = np.tril(np.ones((c, c), np.float32))
    mask[-1] = np.eye(c, dtype=np.float32)
    expo_b = expo[:, ::-1, ::-1]
    mask_b = mask[:, ::-1, ::-1]

    def pack(e):
        e = e.reshape(-1, c)
        return np.concatenate([e, e, e], axis=1)

    return (np.stack([pack(expo), pack(expo_b)]),
            np.stack([mask, mask_b]), len(levels))


def _split3(x):
    hi = x.astype(jnp.bfloat16)
    r1 = x - hi.astype(jnp.float32)
    mid = r1.astype(jnp.bfloat16)
    lo = (r1 - mid.astype(jnp.float32)).astype(jnp.bfloat16)
    return jnp.concatenate([hi, mid, lo], axis=0)


def _hgrn_kernel(q_ref, v_ref, gate_ref, zf_ref, zb_ref, lbf_ref, lbb_ref, gn_ref,
                 expo_ref, mask_ref, o_ref, of_ref, ob_ref, sf_ref, sb_ref,
                 *, chunk, n_levels, out_rows):
    t = q_ref.shape[0]
    c = chunk
    n_chunks = t // c
    nt = (((1,), (1,)), ((), ()))
    tn = (((0,), (0,)), ((), ()))
    sf_ref[...] = jnp.zeros_like(sf_ref)
    sb_ref[...] = jnp.zeros_like(sb_ref)

    def one_chunk(r0, z_ref, lb_ref, st_ref, out_ref, d):
        rows = pl.ds(r0, c)
        z = z_ref[rows, :]
        lb = lb_ref[...]
        e = jnp.exp(-jnp.abs(z))
        r = 1.0 / (1.0 + e)
        er = e * r
        pos = z >= 0.0
        f = lb + (1.0 - lb) * jnp.where(pos, r, er)
        kk = (1.0 - lb) * jnp.where(pos, er, r)
        g = jnp.log(f)
        ex_all = jnp.dot(expo_ref[d], _split3(g), preferred_element_type=jnp.float32)
        b = ex_all[n_levels * c:]
        ex_all = jnp.exp(ex_all)
        qf = q_ref[rows, :].astype(jnp.float32)
        v = v_ref[rows, :]
        att = mask_ref[d, n_levels] * lax.dot_general(
            qf.astype(MXU_DTYPE), kk.astype(MXU_DTYPE), nt,
            preferred_element_type=jnp.float32)
        for li in range(n_levels):
            ex = ex_all[li * c:(li + 1) * c]
            att += mask_ref[d, li] * lax.dot_general(
                (qf * ex).astype(MXU_DTYPE), (kk * ex).astype(MXU_DTYPE), nt,
                preferred_element_type=jnp.float32)
        st = st_ref[...]
        o = jnp.dot(att.astype(MXU_DTYPE), v, preferred_element_type=jnp.float32)
        o += lax.dot_general((qf * ex_all[n_levels * c:]).astype(MXU_DTYPE),
                             st.astype(MXU_DTYPE), nt,
                             preferred_element_type=jnp.float32)
        out_ref[rows, :] = o
        b_last = b[c - 1:c] if d == 0 else b[0:1]
        kd = (kk * jnp.exp(b_last - b)).astype(MXU_DTYPE)
        st_ref[...] = jnp.exp(b_last) * st + lax.dot_general(
            v, kd, tn, preferred_element_type=jnp.float32)

    def body(n, _):
        one_chunk(pl.multiple_of(n * c, c), zf_ref, lbf_ref, sf_ref, of_ref, 0)
        one_chunk(pl.multiple_of((n_chunks - 1 - n) * c, c), zb_ref, lbb_ref, sb_ref, ob_ref, 1)
        return 0

    lax.fori_loop(0, n_chunks, body, 0)

    def finish(i, _):
        rows = pl.ds(pl.multiple_of(i * out_rows, out_rows), out_rows)
        o = _rms(of_ref[rows, :] + ob_ref[rows, :], gn_ref[...])
        gate = gate_ref[rows, :].astype(jnp.float32)
        o_ref[rows, :] = (o * (gate / (1.0 + jnp.exp(-gate)))).astype(o_ref.dtype)
        return 0

    lax.fori_loop(0, t // out_rows, finish, 0)


def _hgrn(proj, zgates, lb, gn):
    b, t, _ = proj.shape
    h = N_HEADS
    c = _tile(t, HG_CHUNK)
    expo, mask, n_levels = _hgrn_constants(c)
    expo = jnp.asarray(expo, MXU_DTYPE)
    mask = jnp.asarray(mask, jnp.float32)
    out_rows = _tile(t, 512)
    head_spec = lambda off: pl.BlockSpec((None, t, HEAD_W), lambda bi, hi: (bi, 0, off + hi))
    vec_spec = pl.BlockSpec((None, 1, HEAD_W), lambda bi, hi: (hi, 0, 0))
    return pl.pallas_call(
        functools.partial(_hgrn_kernel, chunk=c, n_levels=n_levels, out_rows=out_rows),
        grid=(b, h),
        in_specs=[head_spec(3 * h), head_spec(4 * h), head_spec(5 * h),
                  head_spec(0), head_spec(h), vec_spec, vec_spec, vec_spec,
                  pl.BlockSpec(expo.shape, lambda bi, hi: (0, 0, 0)),
                  pl.BlockSpec(mask.shape, lambda bi, hi: (0, 0, 0, 0))],
        out_specs=pl.BlockSpec((None, t, HEAD_W), lambda bi, hi: (bi, 0, hi)),
        out_shape=jax.ShapeDtypeStruct((b, t, h * HEAD_W), MXU_DTYPE),
        scratch_shapes=[pltpu.VMEM((t, HEAD_W), jnp.float32),
                        pltpu.VMEM((t, HEAD_W), jnp.float32),
                        pltpu.VMEM((HEAD_W, HEAD_W), jnp.float32),
                        pltpu.VMEM((HEAD_W, HEAD_W), jnp.float32)],
        compiler_params=_params("parallel", "arbitrary"),
    )(proj, proj, proj, zgates, zgates,
      lb[0].reshape(h, 1, HEAD_W), lb[1].reshape(h, 1, HEAD_W), gn.reshape(h, 1, HEAD_W),
      expo, mask)


def _conv_kernel(bg_ref, c_ref, h_ref, cp_ref, hp_ref, cn_ref, hn_ref, cw_ref, w_ref,
                 r_ref, o_ref, y_ref, u_ref, *, tm, cw, tiles_per_seq):
    i = pl.program_id(0)

    @pl.when(pl.program_id(1) == 0)
    def _():
        first = (i % tiles_per_seq) == 0
        last = (i % tiles_per_seq) == tiles_per_seq - 1
        d = c_ref.shape[1]

        def cols(n, _):
            cs = pl.ds(pl.multiple_of(n * cw, cw), cw)
            f32 = lambda ref, rows: ref[rows, cs].astype(jnp.float32)
            u = f32(c_ref, slice(None)) * f32(h_ref, slice(None))
            last_row = slice(CONV_HALO - 1, CONV_HALO)
            up = f32(cp_ref, last_row) * f32(hp_ref, last_row)
            un = f32(cn_ref, slice(0, 1)) * f32(hn_ref, slice(0, 1))
            u_ref[8:tm + 8, :] = u
            u_ref[7:8, :] = jnp.where(first, 0.0, up)
            u_ref[tm + 8:tm + 9, :] = jnp.where(last, 0.0, un)
            w = cw_ref[:, cs]
            conv = (w[0:1] * u_ref[7:tm + 7, :] + w[1:2] * u
                    + w[2:3] * u_ref[9:tm + 9, :])
            y_ref[:, cs] = (f32(bg_ref, slice(None)) * conv).astype(y_ref.dtype)
            return 0

        lax.fori_loop(0, d // cw, cols, 0)

    o_ref[...] = r_ref[...] + jnp.dot(y_ref[...], w_ref[...],
                                      preferred_element_type=jnp.float32)


def _conv_mixer(proj, conv_w, w_out, res, seq, tm=512, tn=1024, cw=256):
    m, d3 = proj.shape
    d = d3 // 3
    n = w_out.shape[1]
    tm, tn, cw = _tile(seq, tm), _tile(n, tn), _tile(d, cw)
    tiles_per_seq = seq // tm
    hb = tm // CONV_HALO
    n_hb = m // CONV_HALO
    main = lambda sec: pl.BlockSpec((tm, d), lambda i, j: (i, sec))
    prev = lambda sec: pl.BlockSpec((CONV_HALO, d),
                                    lambda i, j: (jnp.maximum(i * hb - 1, 0), sec))
    nxt = lambda sec: pl.BlockSpec((CONV_HALO, d),
                                   lambda i, j: (jnp.minimum((i + 1) * hb, n_hb - 1), sec))
    return pl.pallas_call(
        functools.partial(_conv_kernel, tm=tm, cw=cw, tiles_per_seq=tiles_per_seq),
        grid=(m // tm, n // tn),
        in_specs=[main(0), main(1), main(2), prev(1), prev(2), nxt(1), nxt(2),
                  pl.BlockSpec(conv_w.shape, lambda i, j: (0, 0)),
                  pl.BlockSpec((d, tn), lambda i, j: (0, j)),
                  pl.BlockSpec((tm, tn), lambda i, j: (i, j))],
        out_specs=pl.BlockSpec((tm, tn), lambda i, j: (i, j)),
        out_shape=jax.ShapeDtypeStruct((m, n), jnp.float32),
        scratch_shapes=[pltpu.VMEM((tm, d), MXU_DTYPE),
                        pltpu.VMEM((tm + 16, cw), jnp.float32)],
        compiler_params=_params("parallel", "arbitrary"),
    )(proj, proj, proj, proj, proj, proj, proj, conv_w, w_out, res)


def kernel(x, mix_norm_g, mlp_norm_g, final_norm_g, ab_w_in, ab_w_out, diff_lambda,
           diff_subln_g, hgrn_lb, hgrn_norm_g, conv_w_in, conv_w, conv_w_out,
           mlp_w_up, mlp_w_down):
    bsz, seq, d = x.shape
    m = bsz * seq
    h = N_HEADS
    w_attn = h * HEAD_W
    n_bf = 6 * w_attn
    cast = lambda w: w.astype(MXU_DTYPE)

    hres = x.reshape(m, d)

    w_in = ab_w_in[0]
    proj = _norm_matmul(hres, mix_norm_g[0], cast(w_in[:, :n_bf]), MXU_DTYPE)
    zgates = _norm_matmul(hres, mix_norm_g[0], cast(w_in[:, n_bf:]), jnp.float32)
    proj = proj.reshape(bsz, seq, n_bf)
    zgates = zgates.reshape(bsz, seq, 2 * w_attn)

    lam_init = 0.8 - 0.6 * math.exp(-0.3 * 0)
    slopes = 2.0 ** (-8.0 * jnp.arange(1, h + 1, dtype=jnp.float32) / h)
    slopes = jnp.broadcast_to(slopes[:, None, None], (h, 1, HEAD_W))
    oa = _attention(proj, slopes, diff_lambda[0].astype(jnp.float32), diff_subln_g[0], lam_init)

    lb = jnp.cumsum(jax.nn.softmax(hgrn_lb.astype(jnp.float32), axis=1), axis=1)[:, 0]
    ob = _hgrn(proj, zgates, lb, hgrn_norm_g[0])

    w_out = cast(ab_w_out[0])
    hres = _proj2_residual(oa.reshape(m, w_attn), ob.reshape(m, w_attn),
                           w_out[:w_attn], w_out[w_attn:], hres)
    hres = _mlp(hres, mlp_norm_g[0], cast(mlp_w_up[0]), cast(mlp_w_down[0]),
                final_norm_g, False)

    proj = _norm_matmul(hres, mix_norm_g[1], cast(conv_w_in[0]), MXU_DTYPE)
    hres = _conv_mixer(proj, conv_w[0], cast(conv_w_out[0]), hres, seq)
    hres = _mlp(hres, mlp_norm_g[1], cast(mlp_w_up[1]), cast(mlp_w_down[1]),
                final_norm_g, True)
    return hres.reshape(bsz, seq, d)
```

```python
import functools
import math

import numpy as np
import jax
import jax.numpy as jnp
from jax import lax
from jax.experimental import pallas as pl
from jax.experimental.pallas import tpu as pltpu

EPS = 1e-6
N_HEADS = 8
HEAD_W = 128
DA = 64
HG_CHUNK = 64
CONV_HALO = 16
MXU_DTYPE = jnp.bfloat16
LOG2E = 1.4426950408889634
ATTN_TQ = 256
ATTN_TK = 512
VMEM_LIMIT = 56 * 1024 * 1024


def _tile(dim, pref):
    return pref if dim % pref == 0 else dim


def _rms(x, g):
    ms = jnp.mean(x * x, axis=-1, keepdims=True)
    return x * lax.rsqrt(ms + EPS) * g


def _params(*sem):
    return pltpu.CompilerParams(dimension_semantics=sem, vmem_limit_bytes=VMEM_LIMIT)


def _norm_matmul_kernel(x_ref, g_ref, w_ref, o_ref, hn_ref):
    @pl.when(pl.program_id(1) == 0)
    def _():
        hn_ref[...] = _rms(x_ref[...], g_ref[...]).astype(hn_ref.dtype)

    o_ref[...] = jnp.dot(hn_ref[...], w_ref[...],
                         preferred_element_type=jnp.float32).astype(o_ref.dtype)


def _norm_matmul(x, g, w, out_dtype, tm=1024, tn=1024):
    m, k = x.shape
    n = w.shape[1]
    tm, tn = _tile(m, tm), _tile(n, tn)
    return pl.pallas_call(
        _norm_matmul_kernel,
        grid=(m // tm, n // tn),
        in_specs=[pl.BlockSpec((tm, k), lambda i, j: (i, 0)),
                  pl.BlockSpec((1, k), lambda i, j: (0, 0)),
                  pl.BlockSpec((k, tn), lambda i, j: (0, j))],
        out_specs=pl.BlockSpec((tm, tn), lambda i, j: (i, j)),
        out_shape=jax.ShapeDtypeStruct((m, n), out_dtype),
        scratch_shapes=[pltpu.VMEM((tm, k), MXU_DTYPE)],
        compiler_params=_params("parallel", "arbitrary"),
    )(x, g.reshape(1, k), w)


def _proj2_kernel(a_ref, b_ref, wa_ref, wb_ref, r_ref, o_ref):
    acc = jnp.dot(a_ref[...], wa_ref[...], preferred_element_type=jnp.float32)
    acc += jnp.dot(b_ref[...], wb_ref[...], preferred_element_type=jnp.float32)
    o_ref[...] = r_ref[...] + acc


def _proj2_residual(a, b, wa, wb, res, tm=1024, tn=1024):
    m, ka = a.shape
    kb = b.shape[1]
    n = wa.shape[1]
    tm, tn = _tile(m, tm), _tile(n, tn)
    return pl.pallas_call(
        _proj2_kernel,
        grid=(m // tm, n // tn),
        in_specs=[pl.BlockSpec((tm, ka), lambda i, j: (i, 0)),
                  pl.BlockSpec((tm, kb), lambda i, j: (i, 0)),
                  pl.BlockSpec((ka, tn), lambda i, j: (0, j)),
                  pl.BlockSpec((kb, tn), lambda i, j: (0, j)),
                  pl.BlockSpec((tm, tn), lambda i, j: (i, j))],
        out_specs=pl.BlockSpec((tm, tn), lambda i, j: (i, j)),
        out_shape=jax.ShapeDtypeStruct((m, n), jnp.float32),
        compiler_params=_params("parallel", "arbitrary"),
    )(a, b, wa, wb, res)


def _mlp_kernel(h_ref, g_ref, wu_ref, wd_ref, gf_ref, o_ref, hn_ref, *, final_norm):
    j = pl.program_id(1)

    @pl.when(j == 0)
    def _():
        h = h_ref[...]
        hn_ref[...] = _rms(h, g_ref[...]).astype(hn_ref.dtype)
        o_ref[...] = h

    a = jnp.dot(hn_ref[...], wu_ref[...], preferred_element_type=jnp.float32)
    a = jnp.maximum(a, 0.0)
    a = (a * a).astype(wd_ref.dtype)
    o_ref[...] += jnp.dot(a, wd_ref[...], preferred_element_type=jnp.float32)

    if final_norm:
        @pl.when(j == pl.num_programs(1) - 1)
        def _():
            o_ref[...] = _rms(o_ref[...], gf_ref[...])


def _mlp(h, g, w_up, w_down, g_final, final_norm, tm=1024, tf=512):
    m, d = h.shape
    f = w_up.shape[1]
    tm, tf = _tile(m, tm), _tile(f, tf)
    return pl.pallas_call(
        functools.partial(_mlp_kernel, final_norm=final_norm),
        grid=(m // tm, f // tf),
        in_specs=[pl.BlockSpec((tm, d), lambda i, j: (i, 0)),
                  pl.BlockSpec((1, d), lambda i, j: (0, 0)),
                  pl.BlockSpec((d, tf), lambda i, j: (0, j)),
                  pl.BlockSpec((tf, d), lambda i, j: (j, 0)),
                  pl.BlockSpec((1, d), lambda i, j: (0, 0))],
        out_specs=pl.BlockSpec((tm, d), lambda i, j: (i, 0)),
        out_shape=jax.ShapeDtypeStruct((m, d), jnp.float32),
        scratch_shapes=[pltpu.VMEM((tm, d), MXU_DTYPE)],
        compiler_params=_params("parallel", "arbitrary"),
    )(h, g.reshape(1, d), w_up, w_down, g_final.reshape(1, d))


def _attn_kernel(q_ref, k_ref, v_ref, slope_ref, lam_ref, g_ref, o_ref,
                 s_ref, bias_ref, vaug_ref, m_ref, *, tq, tk, lam_init):
    t = k_ref.shape[0]
    nk = t // tk
    bi, qi = pl.program_id(1), pl.program_id(2)
    q0 = qi * tq
    nt = (((1,), (1,)), ((), ()))
    c = slope_ref[...][:, :1] * LOG2E

    @pl.when((bi == 0) & (qi == 0))
    def _():
        r = lax.broadcasted_iota(jnp.int32, (tq, tk), 0)
        u = lax.broadcasted_iota(jnp.int32, (tq, tk), 1)
        rel = (r - u + t).astype(jnp.float32)

        def fill(a, _):
            u0 = pl.multiple_of(a * tk, tk)
            bias_ref[:, pl.ds(u0, tk)] = -c * jnp.abs(rel - u0.astype(jnp.float32))
            return 0

        lax.fori_loop(0, 2 * t // tk, fill, 0)

    @pl.when(qi == 0)
    def _():
        vaug_ref[:, :HEAD_W] = v_ref[...]
        vaug_ref[:, HEAD_W:] = jnp.ones((t, HEAD_W), vaug_ref.dtype)

    q = q_ref[...].astype(jnp.float32) * (DA ** -0.5 * LOG2E)
    lane = lax.broadcasted_iota(jnp.int32, q.shape, 1)
    qq = jnp.concatenate([jnp.where(lane < DA, q, 0.0),
                          jnp.where(lane >= DA, q, 0.0)], axis=0).astype(k_ref.dtype)

    for j in range(nk):
        k0 = j * tk
        s = lax.dot_general(qq, k_ref[k0:k0 + tk, :], nt,
                            preferred_element_type=jnp.float32)
        bias = bias_ref[:, pl.ds(pl.multiple_of(k0 - q0 + t, HEAD_W), tk)]
        st = s + jnp.concatenate([bias, bias], axis=0)
        s_ref[:, k0:k0 + tk] = st
        part = functools.reduce(
            jnp.maximum, [st[:, a * HEAD_W:(a + 1) * HEAD_W] for a in range(tk // HEAD_W)])
        m_ref[...] = part if j == 0 else jnp.maximum(m_ref[...], part)
    m = jnp.max(m_ref[...], axis=1, keepdims=True)

    acc = [None, None]
    for j in range(nk):
        k0 = j * tk
        for mp in range(2):
            rows = slice(mp * tq, (mp + 1) * tq)
            p = jnp.exp2(s_ref[rows, k0:k0 + tk] - m[rows]).astype(vaug_ref.dtype)
            d = jnp.dot(p, vaug_ref[k0:k0 + tk, :], preferred_element_type=jnp.float32)
            acc[mp] = d if acc[mp] is None else acc[mp] + d

    lp = lam_ref[...]
    lam = (jnp.exp(jnp.sum(lp[0:1] * lp[1:2], axis=1, keepdims=True))
           - jnp.exp(jnp.sum(lp[2:3] * lp[3:4], axis=1, keepdims=True)) + lam_init)
    o = [a[:, :HEAD_W] / a[:, HEAD_W:HEAD_W + 1] for a in acc]
    o = o[0] - lam * o[1]
    o_ref[...] = (_rms(o, g_ref[...]) * (1.0 - lam_init)).astype(o_ref.dtype)


def _attention(proj, slopes, lam_p, subln_g, lam_init):
    b, t, _ = proj.shape
    tq, tk = _tile(t, ATTN_TQ), _tile(t, ATTN_TK)
    h = N_HEADS
    return pl.pallas_call(
        functools.partial(_attn_kernel, tq=tq, tk=tk, lam_init=lam_init),
        grid=(h, b, t // tq),
        in_specs=[pl.BlockSpec((None, tq, HEAD_W), lambda hi, bi, qi: (bi, qi, hi)),
                  pl.BlockSpec((None, t, HEAD_W), lambda hi, bi, qi: (bi, 0, h + hi)),
                  pl.BlockSpec((None, t, HEAD_W), lambda hi, bi, qi: (bi, 0, 2 * h + hi)),
                  pl.BlockSpec((None, 1, HEAD_W), lambda hi, bi, qi: (hi, 0, 0)),
                  pl.BlockSpec(lam_p.shape, lambda hi, bi, qi: (0, 0)),
                  pl.BlockSpec((1, HEAD_W), lambda hi, bi, qi: (0, 0))],
        out_specs=pl.BlockSpec((None, tq, HEAD_W), lambda hi, bi, qi: (bi, qi, hi)),
        out_shape=jax.ShapeDtypeStruct((b, t, h * HEAD_W), MXU_DTYPE),
        scratch_shapes=[pltpu.VMEM((2 * tq, t), jnp.float32),
                        pltpu.VMEM((tq, 2 * t), jnp.float32),
                        pltpu.VMEM((t, 2 * HEAD_W), MXU_DTYPE),
                        pltpu.VMEM((2 * tq, HEAD_W), jnp.float32)],
        compiler_params=_params("parallel", "arbitrary", "arbitrary"),
    )(proj, proj, proj, slopes, lam_p, subln_g.reshape(1, HEAD_W))


def _hgrn_constants(c):
    levels = []
    size = 2
    while size <= c:
        levels.append(size)
        size *= 2
    expo = np.zeros((len(levels) + 1, c, c), np.float32)
    mask = np.zeros((len(levels) + 1, c, c), np.float32)
    for li, size in enumerate(levels):
        half = size // 2
        for t in range(c):
            pos = t % size
            bd = t - pos + half - 1
            if pos >= half:
                expo[li, t, bd + 1:t + 1] = 1.0
                mask[li, t, t - pos:t - pos + half] = 1.0
            else:
                expo[li, t, t + 1:bd + 1] = 1.0
    expo[-1] = np.tril(np.ones((c, c), np.float32))
    mask[-1] = np.eye(c, dtype=np.float32)
    expo_b = expo[:, ::-1, ::-1]
    mask_b = mask[:, ::-1, ::-1]

    def pack(e):
        e = e.reshape(-1, c)
        return np.concatenate([e, e, e], axis=1)

    return (np.stack([pack(expo), pack(expo_b)]),
            np.stack([mask, mask_b]), len(levels))


def _split3(x):
    hi = x.astype(jnp.bfloat16)
    r1 = x - hi.astype(jnp.float32)
    mid = r1.astype(jnp.bfloat16)
    lo = (r1 - mid.astype(jnp.float32)).astype(jnp.bfloat16)
    return jnp.concatenate([hi, mid, lo], axis=0)


def _hgrn_kernel(q_ref, v_ref, gate_ref, zf_ref, zb_ref, lbf_ref, lbb_ref, gn_ref,
                 expo_ref, mask_ref, o_ref, of_ref, ob_ref, sf_ref, sb_ref,
                 *, chunk, n_levels, out_rows):
    t = q_ref.shape[0]
    c = chunk
    n_chunks = t // c
    nt = (((1,), (1,)), ((), ()))
    tn = (((0,), (0,)), ((), ()))
    sf_ref[...] = jnp.zeros_like(sf_ref)
    sb_ref[...] = jnp.zeros_like(sb_ref)

    def one_chunk(r0, z_ref, lb_ref, st_ref, out_ref, d):
        rows = pl.ds(r0, c)
        z = z_ref[rows, :]
        lb = lb_ref[...]
        e = jnp.exp(-jnp.abs(z))
        r = 1.0 / (1.0 + e)
        er = e * r
        pos = z >= 0.0
        f = lb + (1.0 - lb) * jnp.where(pos, r, er)
        kk = (1.0 - lb) * jnp.where(pos, er, r)
        g = jnp.log(f)
        ex_all = jnp.dot(expo_ref[d], _split3(g), preferred_element_type=jnp.float32)
        b = ex_all[n_levels * c:]
        ex_all = jnp.exp(ex_all)
        qf = q_ref[rows, :].astype(jnp.float32)
        v = v_ref[rows, :]
        att = mask_ref[d, n_levels] * lax.dot_general(
            qf.astype(MXU_DTYPE), kk.astype(MXU_DTYPE), nt,
            preferred_element_type=jnp.float32)
        for li in range(n_levels):
            ex = ex_all[li * c:(li + 1) * c]
            att += mask_ref[d, li] * lax.dot_general(
                (qf * ex).astype(MXU_DTYPE), (kk * ex).astype(MXU_DTYPE), nt,
                preferred_element_type=jnp.float32)
        st = st_ref[...]
        o = jnp.dot(att.astype(MXU_DTYPE), v, preferred_element_type=jnp.float32)
        o += lax.dot_general((qf * ex_all[n_levels * c:]).astype(MXU_DTYPE),
                             st.astype(MXU_DTYPE), nt,
                             preferred_element_type=jnp.float32)
        out_ref[rows, :] = o
        b_last = b[c - 1:c] if d == 0 else b[0:1]
        kd = (kk * jnp.exp(b_last - b)).astype(MXU_DTYPE)
        st_ref[...] = jnp.exp(b_last) * st + lax.dot_general(
            v, kd, tn, preferred_element_type=jnp.float32)

    def body(n, _):
        one_chunk(pl.multiple_of(n * c, c), zf_ref, lbf_ref, sf_ref, of_ref, 0)
        one_chunk(pl.multiple_of((n_chunks - 1 - n) * c, c), zb_ref, lbb_ref, sb_ref, ob_ref, 1)
        return 0

    lax.fori_loop(0, n_chunks, body, 0)

    def finish(i, _):
        rows = pl.ds(pl.multiple_of(i * out_rows, out_rows), out_rows)
        o = _rms(of_ref[rows, :] + ob_ref[rows, :], gn_ref[...])
        gate = gate_ref[rows, :].astype(jnp.float32)
        o_ref[rows, :] = (o * (gate / (1.0 + jnp.exp(-gate)))).astype(o_ref.dtype)
        return 0

    lax.fori_loop(0, t // out_rows, finish, 0)


def _hgrn(proj, zgates, lb, gn):
    b, t, _ = proj.shape
    h = N_HEADS
    c = _tile(t, HG_CHUNK)
    expo, mask, n_levels = _hgrn_constants(c)
    expo = jnp.asarray(expo, MXU_DTYPE)
    mask = jnp.asarray(mask, jnp.float32)
    out_rows = _tile(t, 512)
    head_spec = lambda off: pl.BlockSpec((None, t, HEAD_W), lambda bi, hi: (bi, 0, off + hi))
    vec_spec = pl.BlockSpec((None, 1, HEAD_W), lambda bi, hi: (hi, 0, 0))
    return pl.pallas_call(
        functools.partial(_hgrn_kernel, chunk=c, n_levels=n_levels, out_rows=out_rows),
        grid=(b, h),
        in_specs=[head_spec(3 * h), head_spec(4 * h), head_spec(5 * h),
                  head_spec(0), head_spec(h), vec_spec, vec_spec, vec_spec,
                  pl.BlockSpec(expo.shape, lambda bi, hi: (0, 0, 0)),
                  pl.BlockSpec(mask.shape, lambda bi, hi: (0, 0, 0, 0))],
        out_specs=pl.BlockSpec((None, t, HEAD_W), lambda bi, hi: (bi, 0, hi)),
        out_shape=jax.ShapeDtypeStruct((b, t, h * HEAD_W), MXU_DTYPE),
        scratch_shapes=[pltpu.VMEM((t, HEAD_W), jnp.float32),
                        pltpu.VMEM((t, HEAD_W), jnp.float32),
                        pltpu.VMEM((HEAD_W, HEAD_W), jnp.float32),
                        pltpu.VMEM((HEAD_W, HEAD_W), jnp.float32)],
        compiler_params=_params("parallel", "arbitrary"),
    )(proj, proj, proj, zgates, zgates,
      lb[0].reshape(h, 1, HEAD_W), lb[1].reshape(h, 1, HEAD_W), gn.reshape(h, 1, HEAD_W),
      expo, mask)


def _conv_kernel(bg_ref, c_ref, h_ref, cp_ref, hp_ref, cn_ref, hn_ref, cw_ref, w_ref,
                 r_ref, o_ref, y_ref, u_ref, *, tm, cw, tiles_per_seq):
    i = pl.program_id(0)

    @pl.when(pl.program_id(1) == 0)
    def _():
        first = (i % tiles_per_seq) == 0
        last = (i % tiles_per_seq) == tiles_per_seq - 1
        d = c_ref.shape[1]

        def cols(n, _):
            cs = pl.ds(pl.multiple_of(n * cw, cw), cw)
            f32 = lambda ref, rows: ref[rows, cs].astype(jnp.float32)
            u = f32(c_ref, slice(None)) * f32(h_ref, slice(None))
            last_row = slice(CONV_HALO - 1, CONV_HALO)
            up = f32(cp_ref, last_row) * f32(hp_ref, last_row)
            un = f32(cn_ref, slice(0, 1)) * f32(hn_ref, slice(0, 1))
            u_ref[8:tm + 8, :] = u
            u_ref[7:8, :] = jnp.where(first, 0.0, up)
            u_ref[tm + 8:tm + 9, :] = jnp.where(last, 0.0, un)
            w = cw_ref[:, cs]
            conv = (w[0:1] * u_ref[7:tm + 7, :] + w[1:2] * u
                    + w[2:3] * u_ref[9:tm + 9, :])
            y_ref[:, cs] = (f32(bg_ref, slice(None)) * conv).astype(y_ref.dtype)
            return 0

        lax.fori_loop(0, d // cw, cols, 0)

    o_ref[...] = r_ref[...] + jnp.dot(y_ref[...], w_ref[...],
                                      preferred_element_type=jnp.float32)


def _conv_mixer(proj, conv_w, w_out, res, seq, tm=512, tn=1024, cw=256):
    m, d3 = proj.shape
    d = d3 // 3
    n = w_out.shape[1]
    tm, tn, cw = _tile(seq, tm), _tile(n, tn), _tile(d, cw)
    tiles_per_seq = seq // tm
    hb = tm // CONV_HALO
    n_hb = m // CONV_HALO
    main = lambda sec: pl.BlockSpec((tm, d), lambda i, j: (i, sec))
    prev = lambda sec: pl.BlockSpec((CONV_HALO, d),
                                    lambda i, j: (jnp.maximum(i * hb - 1, 0), sec))
    nxt = lambda sec: pl.BlockSpec((CONV_HALO, d),
                                   lambda i, j: (jnp.minimum((i + 1) * hb, n_hb - 1), sec))
    return pl.pallas_call(
        functools.partial(_conv_kernel, tm=tm, cw=cw, tiles_per_seq=tiles_per_seq),
        grid=(m // tm, n // tn),
        in_specs=[main(0), main(1), main(2), prev(1), prev(2), nxt(1), nxt(2),
                  pl.BlockSpec(conv_w.shape, lambda i, j: (0, 0)),
                  pl.BlockSpec((d, tn), lambda i, j: (0, j)),
                  pl.BlockSpec((tm, tn), lambda i, j: (i, j))],
        out_specs=pl.BlockSpec((tm, tn), lambda i, j: (i, j)),
        out_shape=jax.ShapeDtypeStruct((m, n), jnp.float32),
        scratch_shapes=[pltpu.VMEM((tm, d), MXU_DTYPE),
                        pltpu.VMEM((tm + 16, cw), jnp.float32)],
        compiler_params=_params("parallel", "arbitrary"),
    )(proj, proj, proj, proj, proj, proj, proj, conv_w, w_out, res)


def kernel(x, mix_norm_g, mlp_norm_g, final_norm_g, ab_w_in, ab_w_out, diff_lambda,
           diff_subln_g, hgrn_lb, hgrn_norm_g, conv_w_in, conv_w, conv_w_out,
           mlp_w_up, mlp_w_down):
    bsz, seq, d = x.shape
    m = bsz * seq
    h = N_HEADS
    w_attn = h * HEAD_W
    n_bf = 6 * w_attn
    cast = lambda w: w.astype(MXU_DTYPE)

    hres = x.reshape(m, d)

    w_in = ab_w_in[0]
    proj = _norm_matmul(hres, mix_norm_g[0], cast(w_in[:, :n_bf]), MXU_DTYPE)
    zgates = _norm_matmul(hres, mix_norm_g[0], cast(w_in[:, n_bf:]), jnp.float32)
    proj = proj.reshape(bsz, seq, n_bf)
    zgates = zgates.reshape(bsz, seq, 2 * w_attn)

    lam_init = 0.8 - 0.6 * math.exp(-0.3 * 0)
    slopes = 2.0 ** (-8.0 * jnp.arange(1, h + 1, dtype=jnp.float32) / h)
    slopes = jnp.broadcast_to(slopes[:, None, None], (h, 1, HEAD_W))
    oa = _attention(proj, slopes, diff_lambda[0].astype(jnp.float32), diff_subln_g[0], lam_init)

    lb = jnp.cumsum(jax.nn.softmax(hgrn_lb.astype(jnp.float32), axis=1), axis=1)[:, 0]
    ob = _hgrn(proj, zgates, lb, hgrn_norm_g[0])

    w_out = cast(ab_w_out[0])
    hres = _proj2_residual(oa.reshape(m, w_attn), ob.reshape(m, w_attn),
                           w_out[:w_attn], w_out[w_attn:], hres)
    hres = _mlp(hres, mlp_norm_g[0], cast(mlp_w_up[0]), cast(mlp_w_down[0]),
                final_norm_g, False)

    proj = _norm_matmul(hres, mix_norm_g[1], cast(conv_w_in[0]), MXU_DTYPE)
    hres = _conv_mixer(proj, conv_w[0], cast(conv_w_out[0]), hres, seq)
    hres = _mlp(hres, mlp_norm_g[1], cast(mlp_w_up[1]), cast(mlp_w_down[1]),
                final_norm_g, True)
    return hres.reshape(bsz, seq, d)
```

```python
import functools
import math

import numpy as np
import jax
import jax.numpy as jnp
from jax import lax
from jax.experimental import pallas as pl
from jax.experimental.pallas import tpu as pltpu

EPS = 1e-6
N_HEADS = 8
HEAD_W = 128
DA = 64
HG_CHUNK = 64
HG_GROUP = 4
CONV_HALO = 16
MXU_DTYPE = jnp.bfloat16
LOG2E = 1.4426950408889634
ATTN_TQ = 256
ATTN_TK = 512
VMEM_LIMIT = 56 * 1024 * 1024


def _tile(dim, pref):
    return pref if dim % pref == 0 else dim


def _rms(x, g):
    ms = jnp.mean(x * x, axis=-1, keepdims=True)
    return x * lax.rsqrt(ms + EPS) * g


def _params(*sem):
    return pltpu.CompilerParams(dimension_semantics=sem, vmem_limit_bytes=VMEM_LIMIT)


def _norm_matmul_kernel(x_ref, g_ref, w_ref, o_ref, hn_ref):
    @pl.when(pl.program_id(1) == 0)
    def _():
        hn_ref[...] = _rms(x_ref[...], g_ref[...]).astype(hn_ref.dtype)

    o_ref[...] = jnp.dot(hn_ref[...], w_ref[...],
                         preferred_element_type=jnp.float32).astype(o_ref.dtype)


def _norm_matmul(x, g, w, out_dtype, tm=1024, tn=1024):
    m, k = x.shape
    n = w.shape[1]
    tm, tn = _tile(m, tm), _tile(n, tn)
    return pl.pallas_call(
        _norm_matmul_kernel,
        grid=(m // tm, n // tn),
        in_specs=[pl.BlockSpec((tm, k), lambda i, j: (i, 0)),
                  pl.BlockSpec((1, k), lambda i, j: (0, 0)),
                  pl.BlockSpec((k, tn), lambda i, j: (0, j))],
        out_specs=pl.BlockSpec((tm, tn), lambda i, j: (i, j)),
        out_shape=jax.ShapeDtypeStruct((m, n), out_dtype),
        scratch_shapes=[pltpu.VMEM((tm, k), MXU_DTYPE)],
        compiler_params=_params("parallel", "arbitrary"),
    )(x, g.reshape(1, k), w)


def _proj2_kernel(a_ref, b_ref, wa_ref, wb_ref, r_ref, o_ref):
    acc = jnp.dot(a_ref[...], wa_ref[...], preferred_element_type=jnp.float32)
    acc += jnp.dot(b_ref[...], wb_ref[...], preferred_element_type=jnp.float32)
    o_ref[...] = r_ref[...] + acc


def _proj2_residual(a, b, wa, wb, res, tm=1024, tn=1024):
    m, ka = a.shape
    kb = b.shape[1]
    n = wa.shape[1]
    tm, tn = _tile(m, tm), _tile(n, tn)
    return pl.pallas_call(
        _proj2_kernel,
        grid=(m // tm, n // tn),
        in_specs=[pl.BlockSpec((tm, ka), lambda i, j: (i, 0)),
                  pl.BlockSpec((tm, kb), lambda i, j: (i, 0)),
                  pl.BlockSpec((ka, tn), lambda i, j: (0, j)),
                  pl.BlockSpec((kb, tn), lambda i, j: (0, j)),
                  pl.BlockSpec((tm, tn), lambda i, j: (i, j))],
        out_specs=pl.BlockSpec((tm, tn), lambda i, j: (i, j)),
        out_shape=jax.ShapeDtypeStruct((m, n), jnp.float32),
        compiler_params=_params("parallel", "arbitrary"),
    )(a, b, wa, wb, res)


def _mlp_kernel(h_ref, g_ref, wu_ref, wd_ref, gf_ref, o_ref, hn_ref, *, final_norm):
    j = pl.program_id(1)

    @pl.when(j == 0)
    def _():
        h = h_ref[...]
        hn_ref[...] = _rms(h, g_ref[...]).astype(hn_ref.dtype)
        o_ref[...] = h

    a = jnp.dot(hn_ref[...], wu_ref[...], preferred_element_type=jnp.float32)
    a = jnp.maximum(a, 0.0)
    a = (a * a).astype(wd_ref.dtype)
    o_ref[...] += jnp.dot(a, wd_ref[...], preferred_element_type=jnp.float32)

    if final_norm:
        @pl.when(j == pl.num_programs(1) - 1)
        def _():
            o_ref[...] = _rms(o_ref[...], gf_ref[...])


def _mlp(h, g, w_up, w_down, g_final, final_norm, tm=1024, tf=512):
    m, d = h.shape
    f = w_up.shape[1]
    tm, tf = _tile(m, tm), _tile(f, tf)
    return pl.pallas_call(
        functools.partial(_mlp_kernel, final_norm=final_norm),
        grid=(m // tm, f // tf),
        in_specs=[pl.BlockSpec((tm, d), lambda i, j: (i, 0)),
                  pl.BlockSpec((1, d), lambda i, j: (0, 0)),
                  pl.BlockSpec((d, tf), lambda i, j: (0, j)),
                  pl.BlockSpec((tf, d), lambda i, j: (j, 0)),
                  pl.BlockSpec((1, d), lambda i, j: (0, 0))],
        out_specs=pl.BlockSpec((tm, d), lambda i, j: (i, 0)),
        out_shape=jax.ShapeDtypeStruct((m, d), jnp.float32),
        scratch_shapes=[pltpu.VMEM((tm, d), MXU_DTYPE)],
        compiler_params=_params("parallel", "arbitrary"),
    )(h, g.reshape(1, d), w_up, w_down, g_final.reshape(1, d))


def _attn_kernel(q_ref, k_ref, v_ref, slope_ref, lam_ref, g_ref, o_ref,
                 s_ref, bias_ref, vaug_ref, m_ref, *, tq, tk, lam_init):
    t = k_ref.shape[0]
    nk = t // tk
    bi, qi = pl.program_id(1), pl.program_id(2)
    q0 = qi * tq
    nt = (((1,), (1,)), ((), ()))
    c = slope_ref[...][:, :1] * LOG2E

    @pl.when((bi == 0) & (qi == 0))
    def _():
        r = lax.broadcasted_iota(jnp.int32, (tq, tk), 0)
        u = lax.broadcasted_iota(jnp.int32, (tq, tk), 1)
        rel = (r - u + t).astype(jnp.float32)

        def fill(a, _):
            u0 = pl.multiple_of(a * tk, tk)
            bias_ref[:, pl.ds(u0, tk)] = -c * jnp.abs(rel - u0.astype(jnp.float32))
            return 0

        lax.fori_loop(0, 2 * t // tk, fill, 0)

    @pl.when(qi == 0)
    def _():
        vaug_ref[:, :HEAD_W] = v_ref[...]
        vaug_ref[:, HEAD_W:] = jnp.ones((t, HEAD_W), vaug_ref.dtype)

    q = q_ref[...].astype(jnp.float32) * (DA ** -0.5 * LOG2E)
    lane = lax.broadcasted_iota(jnp.int32, q.shape, 1)
    qq = jnp.concatenate([jnp.where(lane < DA, q, 0.0),
                          jnp.where(lane >= DA, q, 0.0)], axis=0).astype(k_ref.dtype)

    for j in range(nk):
        k0 = j * tk
        s = lax.dot_general(qq, k_ref[k0:k0 + tk, :], nt,
                            preferred_element_type=jnp.float32)
        bias = bias_ref[:, pl.ds(pl.multiple_of(k0 - q0 + t, HEAD_W), tk)]
        st = s + jnp.concatenate([bias, bias], axis=0)
        s_ref[:, k0:k0 + tk] = st
        part = functools.reduce(
            jnp.maximum, [st[:, a * HEAD_W:(a + 1) * HEAD_W] for a in range(tk // HEAD_W)])
        m_ref[...] = part if j == 0 else jnp.maximum(m_ref[...], part)
    m = jnp.max(m_ref[...], axis=1, keepdims=True)

    acc = [None, None]
    for j in range(nk):
        k0 = j * tk
        for mp in range(2):
            rows = slice(mp * tq, (mp + 1) * tq)
            p = jnp.exp2(s_ref[rows, k0:k0 + tk] - m[rows]).astype(vaug_ref.dtype)
            d = jnp.dot(p, vaug_ref[k0:k0 + tk, :], preferred_element_type=jnp.float32)
            acc[mp] = d if acc[mp] is None else acc[mp] + d

    lp = lam_ref[...]
    lam = (jnp.exp(jnp.sum(lp[0:1] * lp[1:2], axis=1, keepdims=True))
           - jnp.exp(jnp.sum(lp[2:3] * lp[3:4], axis=1, keepdims=True)) + lam_init)
    o = [a[:, :HEAD_W] / a[:, HEAD_W:HEAD_W + 1] for a in acc]
    o = o[0] - lam * o[1]
    o_ref[...] = (_rms(o, g_ref[...]) * (1.0 - lam_init)).astype(o_ref.dtype)


def _attention(proj, slopes, lam_p, subln_g, lam_init):
    b, t, _ = proj.shape
    tq, tk = _tile(t, ATTN_TQ), _tile(t, ATTN_TK)
    h = N_HEADS
    return pl.pallas_call(
        functools.partial(_attn_kernel, tq=tq, tk=tk, lam_init=lam_init),
        grid=(h, b, t // tq),
        in_specs=[pl.BlockSpec((None, tq, HEAD_W), lambda hi, bi, qi: (bi, qi, hi)),
                  pl.BlockSpec((None, t, HEAD_W), lambda hi, bi, qi: (bi, 0, h + hi)),
                  pl.BlockSpec((None, t, HEAD_W), lambda hi, bi, qi: (bi, 0, 2 * h + hi)),
                  pl.BlockSpec((None, 1, HEAD_W), lambda hi, bi, qi: (hi, 0, 0)),
                  pl.BlockSpec(lam_p.shape, lambda hi, bi, qi: (0, 0)),
                  pl.BlockSpec((1, HEAD_W), lambda hi, bi, qi: (0, 0))],
        out_specs=pl.BlockSpec((None, tq, HEAD_W), lambda hi, bi, qi: (bi, qi, hi)),
        out_shape=jax.ShapeDtypeStruct((b, t, h * HEAD_W), MXU_DTYPE),
        scratch_shapes=[pltpu.VMEM((2 * tq, t), jnp.float32),
                        pltpu.VMEM((tq, 2 * t), jnp.float32),
                        pltpu.VMEM((t, 2 * HEAD_W), MXU_DTYPE),
                        pltpu.VMEM((2 * tq, HEAD_W), jnp.float32)],
        compiler_params=_params("parallel", "arbitrary", "arbitrary"),
    )(proj, proj, proj, slopes, lam_p, subln_g.reshape(1, HEAD_W))


def _hgrn_constants(c):
    levels = []
    size = 2
    while size <= c:
        levels.append(size)
        size *= 2
    expo = np.zeros((len(levels) + 1, c, c), np.float32)
    mask = np.zeros((len(levels) + 1, c, c), np.float32)
    for li, size in enumerate(levels):
        half = size // 2
        for t in range(c):
            pos = t % size
            bd = t - pos + half - 1
            if pos >= half:
                expo[li, t, bd + 1:t + 1] = 1.0
                mask[li, t, t - pos:t - pos + half] = 1.0
            else:
                expo[li, t, t + 1:bd + 1] = 1.0
    expo[-1] = np.tril(np.ones((c, c), np.float32))
    mask[-1] = np.eye(c, dtype=np.float32)

    def pack(e):
        e = e.reshape(-1, c)
        return np.concatenate([e, e, e], axis=1)

    return (np.stack([pack(expo), pack(expo[:, ::-1, ::-1])]),
            np.stack([mask, mask[:, ::-1, ::-1]]), len(levels))


def _split3(x):
    hi = x.astype(jnp.bfloat16)
    r1 = x - hi.astype(jnp.float32)
    mid = r1.astype(jnp.bfloat16)
    lo = (r1 - mid.astype(jnp.float32)).astype(jnp.bfloat16)
    return jnp.concatenate([hi, mid, lo], axis=0)


def _hgrn_kernel(q_ref, v_ref, gate_ref, zf_ref, zb_ref, lbf_ref, lbb_ref, gn_ref,
                 expo_ref, mask_ref, o_ref, of_ref, ob_ref, sf_ref, sb_ref,
                 *, chunk, group, n_levels, out_rows):
    t = q_ref.shape[0]
    c = chunk
    n_chunks = t // c
    nt = (((1,), (1,)), ((), ()))
    tn = (((0,), (0,)), ((), ()))
    sf_ref[...] = jnp.zeros_like(sf_ref)
    sb_ref[...] = jnp.zeros_like(sb_ref)

    def chunk_local(r0, z_ref, lb_ref, d):
        rows = pl.ds(r0, c)
        z = z_ref[rows, :]
        lb = lb_ref[...]
        e = jnp.exp(-jnp.abs(z))
        r = 1.0 / (1.0 + e)
        er = e * r
        pos = z >= 0.0
        f = lb + (1.0 - lb) * jnp.where(pos, r, er)
        kk = (1.0 - lb) * jnp.where(pos, er, r)
        g = jnp.log(f)
        ex_all = jnp.dot(expo_ref[d], _split3(g), preferred_element_type=jnp.float32)
        b = ex_all[n_levels * c:]
        ex_all = jnp.exp(ex_all)
        qf = q_ref[rows, :].astype(jnp.float32)
        v = v_ref[rows, :]
        att = mask_ref[d, n_levels] * lax.dot_general(
            qf.astype(MXU_DTYPE), kk.astype(MXU_DTYPE), nt,
            preferred_element_type=jnp.float32)
        for li in range(n_levels):
            ex = ex_all[li * c:(li + 1) * c]
            att += mask_ref[d, li] * lax.dot_general(
                (qf * ex).astype(MXU_DTYPE), (kk * ex).astype(MXU_DTYPE), nt,
                preferred_element_type=jnp.float32)
        intra = jnp.dot(att.astype(MXU_DTYPE), v, preferred_element_type=jnp.float32)
        q_dec = (qf * ex_all[n_levels * c:]).astype(MXU_DTYPE)
        b_last = b[c - 1:c] if d == 0 else b[0:1]
        kd = (kk * jnp.exp(b_last - b)).astype(MXU_DTYPE)
        update = lax.dot_general(v, kd, tn, preferred_element_type=jnp.float32)
        return rows, intra, q_dec, jnp.exp(b_last), update

    def direction(starts, z_ref, lb_ref, st_ref, out_ref, d):
        local = [chunk_local(r0, z_ref, lb_ref, d) for r0 in starts]
        st = st_ref[...]
        for rows, intra, q_dec, decay, update in local:
            out_ref[rows, :] = intra + lax.dot_general(
                q_dec, st.astype(MXU_DTYPE), nt, preferred_element_type=jnp.float32)
            st = decay * st + update
        st_ref[...] = st

    def body(n, _):
        fwd = [pl.multiple_of((n * group + i) * c, c) for i in range(group)]
        bwd = [pl.multiple_of((n_chunks - 1 - n * group - i) * c, c) for i in range(group)]
        direction(fwd, zf_ref, lbf_ref, sf_ref, of_ref, 0)
        direction(bwd, zb_ref, lbb_ref, sb_ref, ob_ref, 1)
        return 0

    lax.fori_loop(0, n_chunks // group, body, 0)

    def finish(i, _):
        rows = pl.ds(pl.multiple_of(i * out_rows, out_rows), out_rows)
        o = _rms(of_ref[rows, :] + ob_ref[rows, :], gn_ref[...])
        gate = gate_ref[rows, :].astype(jnp.float32)
        o_ref[rows, :] = (o * (gate / (1.0 + jnp.exp(-gate)))).astype(o_ref.dtype)
        return 0

    lax.fori_loop(0, t // out_rows, finish, 0)


def _hgrn(proj, zgates, lb, gn):
    b, t, _ = proj.shape
    h = N_HEADS
    c = _tile(t, HG_CHUNK)
    expo, mask, n_levels = _hgrn_constants(c)
    expo = jnp.asarray(expo, MXU_DTYPE)
    mask = jnp.asarray(mask, jnp.float32)
    out_rows = _tile(t, 512)
    group = HG_GROUP if (t // c) % HG_GROUP == 0 else 1
    head_spec = lambda off: pl.BlockSpec((None, t, HEAD_W), lambda bi, hi: (bi, 0, off + hi))
    vec_spec = pl.BlockSpec((None, 1, HEAD_W), lambda bi, hi: (hi, 0, 0))
    return pl.pallas_call(
        functools.partial(_hgrn_kernel, chunk=c, group=group, n_levels=n_levels,
                          out_rows=out_rows),
        grid=(b, h),
        in_specs=[head_spec(3 * h), head_spec(4 * h), head_spec(5 * h),
                  head_spec(0), head_spec(h), vec_spec, vec_spec, vec_spec,
                  pl.BlockSpec(expo.shape, lambda bi, hi: (0, 0, 0)),
                  pl.BlockSpec(mask.shape, lambda bi, hi: (0, 0, 0, 0))],
        out_specs=pl.BlockSpec((None, t, HEAD_W), lambda bi, hi: (bi, 0, hi)),
        out_shape=jax.ShapeDtypeStruct((b, t, h * HEAD_W), MXU_DTYPE),
        scratch_shapes=[pltpu.VMEM((t, HEAD_W), jnp.float32),
                        pltpu.VMEM((t, HEAD_W), jnp.float32),
                        pltpu.VMEM((HEAD_W, HEAD_W), jnp.float32),
                        pltpu.VMEM((HEAD_W, HEAD_W), jnp.float32)],
        compiler_params=_params("parallel", "arbitrary"),
    )(proj, proj, proj, zgates, zgates,
      lb[0].reshape(h, 1, HEAD_W), lb[1].reshape(h, 1, HEAD_W), gn.reshape(h, 1, HEAD_W),
      expo, mask)


def _conv_kernel(bg_ref, c_ref, h_ref, cp_ref, hp_ref, cn_ref, hn_ref, cw_ref, w_ref,
                 r_ref, o_ref, y_ref, u_ref, *, tm, cw, tiles_per_seq):
    i = pl.program_id(0)

    @pl.when(pl.program_id(1) == 0)
    def _():
        first = (i % tiles_per_seq) == 0
        last = (i % tiles_per_seq) == tiles_per_seq - 1
        d = c_ref.shape[1]

        def cols(n, _):
            cs = pl.ds(pl.multiple_of(n * cw, cw), cw)
            f32 = lambda ref, rows: ref[rows, cs].astype(jnp.float32)
            u = f32(c_ref, slice(None)) * f32(h_ref, slice(None))
            last_row = slice(CONV_HALO - 1, CONV_HALO)
            up = f32(cp_ref, last_row) * f32(hp_ref, last_row)
            un = f32(cn_ref, slice(0, 1)) * f32(hn_ref, slice(0, 1))
            u_ref[8:tm + 8, :] = u
            u_ref[7:8, :] = jnp.where(first, 0.0, up)
            u_ref[tm + 8:tm + 9, :] = jnp.where(last, 0.0, un)
            w = cw_ref[:, cs]
            conv = (w[0:1] * u_ref[7:tm + 7, :] + w[1:2] * u
                    + w[2:3] * u_ref[9:tm + 9, :])
            y_ref[:, cs] = (f32(bg_ref, slice(None)) * conv).astype(y_ref.dtype)
            return 0

        lax.fori_loop(0, d // cw, cols, 0)

    o_ref[...] = r_ref[...] + jnp.dot(y_ref[...], w_ref[...],
                                      preferred_element_type=jnp.float32)


def _conv_mixer(proj, conv_w, w_out, res, seq, tm=512, tn=1024, cw=256):
    m, d3 = proj.shape
    d = d3 // 3
    n = w_out.shape[1]
    tm, tn, cw = _tile(seq, tm), _tile(n, tn), _tile(d, cw)
    tiles_per_seq = seq // tm
    hb = tm // CONV_HALO
    n_hb = m // CONV_HALO
    main = lambda sec: pl.BlockSpec((tm, d), lambda i, j: (i, sec))
    prev = lambda sec: pl.BlockSpec((CONV_HALO, d),
                                    lambda i, j: (jnp.maximum(i * hb - 1, 0), sec))
    nxt = lambda sec: pl.BlockSpec((CONV_HALO, d),
                                   lambda i, j: (jnp.minimum((i + 1) * hb, n_hb - 1), sec))
    return pl.pallas_call(
        functools.partial(_conv_kernel, tm=tm, cw=cw, tiles_per_seq=tiles_per_seq),
        grid=(m // tm, n // tn),
        in_specs=[main(0), main(1), main(2), prev(1), prev(2), nxt(1), nxt(2),
                  pl.BlockSpec(conv_w.shape, lambda i, j: (0, 0)),
                  pl.BlockSpec((d, tn), lambda i, j: (0, j)),
                  pl.BlockSpec((tm, tn), lambda i, j: (i, j))],
        out_specs=pl.BlockSpec((tm, tn), lambda i, j: (i, j)),
        out_shape=jax.ShapeDtypeStruct((m, n), jnp.float32),
        scratch_shapes=[pltpu.VMEM((tm, d), MXU_DTYPE),
                        pltpu.VMEM((tm + 16, cw), jnp.float32)],
        compiler_params=_params("parallel", "arbitrary"),
    )(proj, proj, proj, proj, proj, proj, proj, conv_w, w_out, res)


def kernel(x, mix_norm_g, mlp_norm_g, final_norm_g, ab_w_in, ab_w_out, diff_lambda,
           diff_subln_g, hgrn_lb, hgrn_norm_g, conv_w_in, conv_w, conv_w_out,
           mlp_w_up, mlp_w_down):
    bsz, seq, d = x.shape
    m = bsz * seq
    h = N_HEADS
    w_attn = h * HEAD_W
    n_bf = 6 * w_attn
    cast = lambda w: w.astype(MXU_DTYPE)

    hres = x.reshape(m, d)

    w_in = ab_w_in[0]
    proj = _norm_matmul(hres, mix_norm_g[0], cast(w_in[:, :n_bf]), MXU_DTYPE)
    zgates = _norm_matmul(hres, mix_norm_g[0], cast(w_in[:, n_bf:]), jnp.float32)
    proj = proj.reshape(bsz, seq, n_bf)
    zgates = zgates.reshape(bsz, seq, 2 * w_attn)

    lam_init = 0.8 - 0.6 * math.exp(-0.3 * 0)
    slopes = 2.0 ** (-8.0 * jnp.arange(1, h + 1, dtype=jnp.float32) / h)
    slopes = jnp.broadcast_to(slopes[:, None, None], (h, 1, HEAD_W))
    oa = _attention(proj, slopes, diff_lambda[0].astype(jnp.float32), diff_subln_g[0], lam_init)

    lb = jnp.cumsum(jax.nn.softmax(hgrn_lb.astype(jnp.float32), axis=1), axis=1)[:, 0]
    ob = _hgrn(proj, zgates, lb, hgrn_norm_g[0])

    w_out = cast(ab_w_out[0])
    hres = _proj2_residual(oa.reshape(m, w_attn), ob.reshape(m, w_attn),
                           w_out[:w_attn], w_out[w_attn:], hres)
    hres = _mlp(hres, mlp_norm_g[0], cast(mlp_w_up[0]), cast(mlp_w_down[0]),
                final_norm_g, False)

    proj = _norm_matmul(hres, mix_norm_g[1], cast(conv_w_in[0]), MXU_DTYPE)
    hres = _conv_mixer(proj, conv_w[0], cast(conv_w_out[0]), hres, seq)
    hres = _mlp(hres, mlp_norm_g[1], cast(mlp_w_up[1]), cast(mlp_w_down[1]),
                final_norm_g, True)
    return hres.reshape(bsz, seq, d)
```

```python
import functools
import math

import numpy as np
import jax
import jax.numpy as jnp
from jax import lax
from jax.experimental import pallas as pl
from jax.experimental.pallas import tpu as pltpu

EPS = 1e-6
N_HEADS = 8
HEAD_W = 128
DA = 64
HG_CHUNK = 64
HG_GROUP = 4
CONV_HALO = 16
MXU_DTYPE = jnp.bfloat16
LOG2E = 1.4426950408889634
ATTN_TQ = 256
ATTN_TK = 512
ATTN_UNROLL = 4
ATTN_UNDERFLOW = 160.0
VMEM_LIMIT = 56 * 1024 * 1024


def _tile(dim, pref):
    return pref if dim % pref == 0 else dim


def _rms(x, g):
    ms = jnp.mean(x * x, axis=-1, keepdims=True)
    return x * lax.rsqrt(ms + EPS) * g


def _params(*sem):
    return pltpu.CompilerParams(dimension_semantics=sem, vmem_limit_bytes=VMEM_LIMIT)


def _norm_matmul_kernel(x_ref, g_ref, w_ref, o_ref, hn_ref):
    @pl.when(pl.program_id(1) == 0)
    def _():
        hn_ref[...] = _rms(x_ref[...], g_ref[...]).astype(hn_ref.dtype)

    o_ref[...] = jnp.dot(hn_ref[...], w_ref[...],
                         preferred_element_type=jnp.float32).astype(o_ref.dtype)


def _norm_matmul(x, g, w, out_dtype, tm=1024, tn=1024):
    m, k = x.shape
    n = w.shape[1]
    tm, tn = _tile(m, tm), _tile(n, tn)
    return pl.pallas_call(
        _norm_matmul_kernel,
        grid=(m // tm, n // tn),
        in_specs=[pl.BlockSpec((tm, k), lambda i, j: (i, 0)),
                  pl.BlockSpec((1, k), lambda i, j: (0, 0)),
                  pl.BlockSpec((k, tn), lambda i, j: (0, j))],
        out_specs=pl.BlockSpec((tm, tn), lambda i, j: (i, j)),
        out_shape=jax.ShapeDtypeStruct((m, n), out_dtype),
        scratch_shapes=[pltpu.VMEM((tm, k), MXU_DTYPE)],
        compiler_params=_params("parallel", "arbitrary"),
    )(x, g.reshape(1, k), w)


def _proj2_kernel(a_ref, b_ref, wa_ref, wb_ref, r_ref, o_ref):
    acc = jnp.dot(a_ref[...], wa_ref[...], preferred_element_type=jnp.float32)
    acc += jnp.dot(b_ref[...], wb_ref[...], preferred_element_type=jnp.float32)
    o_ref[...] = r_ref[...] + acc


def _proj2_residual(a, b, wa, wb, res, tm=1024, tn=1024):
    m, ka = a.shape
    kb = b.shape[1]
    n = wa.shape[1]
    tm, tn = _tile(m, tm), _tile(n, tn)
    return pl.pallas_call(
        _proj2_kernel,
        grid=(m // tm, n // tn),
        in_specs=[pl.BlockSpec((tm, ka), lambda i, j: (i, 0)),
                  pl.BlockSpec((tm, kb), lambda i, j: (i, 0)),
                  pl.BlockSpec((ka, tn), lambda i, j: (0, j)),
                  pl.BlockSpec((kb, tn), lambda i, j: (0, j)),
                  pl.BlockSpec((tm, tn), lambda i, j: (i, j))],
        out_specs=pl.BlockSpec((tm, tn), lambda i, j: (i, j)),
        out_shape=jax.ShapeDtypeStruct((m, n), jnp.float32),
        compiler_params=_params("parallel", "arbitrary"),
    )(a, b, wa, wb, res)


def _mlp_kernel(h_ref, g_ref, wu_ref, wd_ref, gf_ref, o_ref, hn_ref, *, final_norm):
    j = pl.program_id(1)

    @pl.when(j == 0)
    def _():
        h = h_ref[...]
        hn_ref[...] = _rms(h, g_ref[...]).astype(hn_ref.dtype)
        o_ref[...] = h

    a = jnp.dot(hn_ref[...], wu_ref[...], preferred_element_type=jnp.float32)
    a = jnp.maximum(a, 0.0)
    a = (a * a).astype(wd_ref.dtype)
    o_ref[...] += jnp.dot(a, wd_ref[...], preferred_element_type=jnp.float32)

    if final_norm:
        @pl.when(j == pl.num_programs(1) - 1)
        def _():
            o_ref[...] = _rms(o_ref[...], gf_ref[...])


def _mlp(h, g, w_up, w_down, g_final, final_norm, tm=1024, tf=512):
    m, d = h.shape
    f = w_up.shape[1]
    tm, tf = _tile(m, tm), _tile(f, tf)
    return pl.pallas_call(
        functools.partial(_mlp_kernel, final_norm=final_norm),
        grid=(m // tm, f // tf),
        in_specs=[pl.BlockSpec((tm, d), lambda i, j: (i, 0)),
                  pl.BlockSpec((1, d), lambda i, j: (0, 0)),
                  pl.BlockSpec((d, tf), lambda i, j: (0, j)),
                  pl.BlockSpec((tf, d), lambda i, j: (j, 0)),
                  pl.BlockSpec((1, d), lambda i, j: (0, 0))],
        out_specs=pl.BlockSpec((tm, d), lambda i, j: (i, 0)),
        out_shape=jax.ShapeDtypeStruct((m, d), jnp.float32),
        scratch_shapes=[pltpu.VMEM((tm, d), MXU_DTYPE)],
        compiler_params=_params("parallel", "arbitrary"),
    )(h, g.reshape(1, d), w_up, w_down, g_final.reshape(1, d))


def _attn_kernel(q_ref, qall_ref, k_ref, v_ref, slope_ref, lam_ref, g_ref, o_ref,
                 s0_ref, s1_ref, m_ref, mfin0_ref, mfin1_ref, acc_ref, bias_ref, vaug_ref,
                 qq_ref, win_ref, *, tq, tk, unroll, lam_init):
    t = k_ref.shape[0]
    nk, nq = t // tk, t // tq
    rows = 2 * tq
    bi, qi = pl.program_id(1), pl.program_id(2)
    q0 = qi * tq
    nt = (((1,), (1,)), ((), ()))
    c = slope_ref[...][:, :1] * LOG2E
    s_refs, mfin_refs = (s0_ref, s1_ref), (mfin0_ref, mfin1_ref)

    @pl.when((bi == 0) & (qi == 0))
    def _():
        r = lax.broadcasted_iota(jnp.int32, (tq, tk), 0)
        u = lax.broadcasted_iota(jnp.int32, (tq, tk), 1)
        rel = (r - u + t).astype(jnp.float32)

        def fill(a, _):
            u0 = pl.multiple_of(a * tk, tk)
            bias_ref[:, pl.ds(u0, tk)] = -c * jnp.abs(rel - u0.astype(jnp.float32))
            return 0

        lax.fori_loop(0, 2 * t // tk, fill, 0)

    @pl.when(qi == 0)
    def _():
        vaug_ref[:, :HEAD_W] = v_ref[...]
        vaug_ref[:, HEAD_W:] = jnp.ones((t, HEAD_W), vaug_ref.dtype)

        def max_sq_norm(ref):
            x = ref[...].astype(jnp.float32)
            return jnp.max(jnp.sum(x * x, axis=1, keepdims=True), axis=0, keepdims=True)

        bound = jnp.sqrt(max_sq_norm(qall_ref) * max_sq_norm(k_ref)) * (
            DA ** -0.5 * LOG2E * 1.01)
        dist = jnp.minimum((2.0 * bound + ATTN_UNDERFLOW) / c, 2.0 * t)
        n_win = jnp.floor((2.0 * dist + (tq - 1)) / tk) + 2.0
        n_win = jnp.minimum(jnp.ceil(n_win / unroll) * unroll, 1.0 * nk)
        win_ref[0] = n_win[0, 0].astype(jnp.int32)
        win_ref[1] = jnp.floor(dist)[0, 0].astype(jnp.int32)

    n_win, dist = win_ref[0], win_ref[1]

    def first_tile(start):
        return jnp.clip(jnp.maximum(start - dist, 0) // tk, 0, nk - n_win)

    lo_cur = first_tile(q0)
    lo_prev = first_tile(q0 - tq)

    def sweep1(cur, r):
        k0 = pl.multiple_of((lo_cur + r) * tk, tk)
        s = lax.dot_general(qq_ref[...], k_ref[pl.ds(k0, tk), :], nt,
                            preferred_element_type=jnp.float32)
        bias = bias_ref[:, pl.ds(pl.multiple_of(k0 - q0 + t, HEAD_W), tk)]
        st = s + jnp.concatenate([bias, bias], axis=0)
        s_refs[cur][:, pl.ds(pl.multiple_of(r * tk, tk), tk)] = st
        part = functools.reduce(
            jnp.maximum, [st[:, a * HEAD_W:(a + 1) * HEAD_W] for a in range(tk // HEAD_W)])
        m_ref[...] = jnp.maximum(m_ref[...], part)

    def sweep2(prev, r):
        k0 = pl.multiple_of((lo_prev + r) * tk, tk)
        cols = pl.ds(pl.multiple_of(r * tk, tk), tk)
        for mp in range(2):
            rs = slice(mp * tq, (mp + 1) * tq)
            m_rep = mfin_refs[prev][rs, :]
            p = jnp.exp2(s_refs[prev][rs, cols] - jnp.concatenate([m_rep] * (tk // HEAD_W), axis=1))
            acc_ref[mp] += jnp.dot(p.astype(vaug_ref.dtype), vaug_ref[pl.ds(k0, tk), :],
                                   preferred_element_type=jnp.float32)

    def start_sweep1():
        q = q_ref[...].astype(jnp.float32) * (DA ** -0.5 * LOG2E)
        lane = lax.broadcasted_iota(jnp.int32, q.shape, 1)
        qq_ref[...] = jnp.concatenate([jnp.where(lane < DA, q, 0.0),
                                       jnp.where(lane >= DA, q, 0.0)],
                                      axis=0).astype(qq_ref.dtype)
        m_ref[...] = jnp.full(m_ref.shape, -jnp.inf, jnp.float32)

    def end_sweep1(cur):
        m = jnp.max(m_ref[...], axis=1, keepdims=True)
        mfin_refs[cur][...] = jnp.broadcast_to(m, (rows, HEAD_W))

    def start_sweep2():
        acc_ref[...] = jnp.zeros(acc_ref.shape, jnp.float32)

    def end_sweep2():
        lp = lam_ref[...]
        lam = (jnp.exp(jnp.sum(lp[0:1] * lp[1:2], axis=1, keepdims=True))
               - jnp.exp(jnp.sum(lp[2:3] * lp[3:4], axis=1, keepdims=True)) + lam_init)
        o = [acc_ref[mp][:, :HEAD_W] / acc_ref[mp][:, HEAD_W:HEAD_W + 1] for mp in range(2)]
        o = o[0] - lam * o[1]
        o_ref[...] = (_rms(o, g_ref[...]) * (1.0 - lam_init)).astype(o_ref.dtype)

    def loop(*parts):
        def body(i, _):
            for part in parts:
                for u in range(unroll):
                    part(i * unroll + u)
            return 0
        lax.fori_loop(0, n_win // unroll, body, 0)

    @pl.when(qi == 0)
    def _():
        start_sweep1()
        loop(functools.partial(sweep1, 0))
        end_sweep1(0)

    for cur in range(2):
        @pl.when((qi > 0) & (qi < nq) & (qi % 2 == cur))
        def _():
            start_sweep1()
            start_sweep2()
            loop(functools.partial(sweep1, cur), functools.partial(sweep2, 1 - cur))
            end_sweep1(cur)
            end_sweep2()

    @pl.when(qi == nq)
    def _():
        start_sweep2()
        loop(functools.partial(sweep2, (nq - 1) % 2))
        end_sweep2()


def _attention(proj, slopes, lam_p, subln_g, lam_init):
    b, t, _ = proj.shape
    tq, tk = _tile(t, ATTN_TQ), _tile(t, ATTN_TK)
    nq = t // tq
    h = N_HEADS
    return pl.pallas_call(
        functools.partial(_attn_kernel, tq=tq, tk=tk, lam_init=lam_init,
                          unroll=ATTN_UNROLL if (t // tk) % ATTN_UNROLL == 0 else 1),
        grid=(h, b, nq + 1),
        in_specs=[pl.BlockSpec((None, tq, HEAD_W),
                               lambda hi, bi, qi: (bi, jnp.minimum(qi, nq - 1), hi)),
                  pl.BlockSpec((None, t, HEAD_W), lambda hi, bi, qi: (bi, 0, hi)),
                  pl.BlockSpec((None, t, HEAD_W), lambda hi, bi, qi: (bi, 0, h + hi)),
                  pl.BlockSpec((None, t, HEAD_W), lambda hi, bi, qi: (bi, 0, 2 * h + hi)),
                  pl.BlockSpec((None, 1, HEAD_W), lambda hi, bi, qi: (hi, 0, 0)),
                  pl.BlockSpec(lam_p.shape, lambda hi, bi, qi: (0, 0)),
                  pl.BlockSpec((1, HEAD_W), lambda hi, bi, qi: (0, 0))],
        out_specs=pl.BlockSpec((None, tq, HEAD_W),
                               lambda hi, bi, qi: (bi, jnp.maximum(qi - 1, 0), hi)),
        out_shape=jax.ShapeDtypeStruct((b, t, h * HEAD_W), MXU_DTYPE),
        scratch_shapes=[pltpu.VMEM((2 * tq, t), jnp.float32),
                        pltpu.VMEM((2 * tq, t), jnp.float32),
                        pltpu.VMEM((2 * tq, HEAD_W), jnp.float32),
                        pltpu.VMEM((2 * tq, HEAD_W), jnp.float32),
                        pltpu.VMEM((2 * tq, HEAD_W), jnp.float32),
                        pltpu.VMEM((2, tq, 2 * HEAD_W), jnp.float32),
                        pltpu.VMEM((tq, 2 * t), jnp.float32),
                        pltpu.VMEM((t, 2 * HEAD_W), MXU_DTYPE),
                        pltpu.VMEM((2 * tq, HEAD_W), MXU_DTYPE),
                        pltpu.SMEM((2,), jnp.int32)],
        compiler_params=_params("parallel", "arbitrary", "arbitrary"),
    )(proj, proj, proj, proj, slopes, lam_p, subln_g.reshape(1, HEAD_W))


def _hgrn_constants(c):
    levels = []
    size = 2
    while size <= c:
        levels.append(size)
        size *= 2
    expo = np.zeros((len(levels) + 1, c, c), np.float32)
    mask = np.zeros((len(levels) + 1, c, c), np.float32)
    for li, size in enumerate(levels):
        half = size // 2
        for t in range(c):
            pos = t % size
            bd = t - pos + half - 1
            if pos >= half:
                expo[li, t, bd + 1:t + 1] = 1.0
                mask[li, t, t - pos:t - pos + half] = 1.0
            else:
                expo[li, t, t + 1:bd + 1] = 1.0
    expo[-1] = np.tril(np.ones((c, c), np.float32))
    mask[-1] = np.eye(c, dtype=np.float32)

    def pack(e):
        e = e.reshape(-1, c)
        return np.concatenate([e, e, e], axis=1)

    return (np.stack([pack(expo), pack(expo[:, ::-1, ::-1])]),
            np.stack([mask, mask[:, ::-1, ::-1]]), len(levels))


def _split3(x):
    hi = x.astype(jnp.bfloat16)
    r1 = x - hi.astype(jnp.float32)
    mid = r1.astype(jnp.bfloat16)
    lo = (r1 - mid.astype(jnp.float32)).astype(jnp.bfloat16)
    return jnp.concatenate([hi, mid, lo], axis=0)


def _hgrn_kernel(q_ref, v_ref, gate_ref, zf_ref, zb_ref, lbf_ref, lbb_ref, gn_ref,
                 expo_ref, mask_ref, o_ref, of_ref, ob_ref, sf_ref, sb_ref,
                 *, chunk, group, n_levels, out_rows):
    t = q_ref.shape[0]
    c = chunk
    n_chunks = t // c
    nt = (((1,), (1,)), ((), ()))
    tn = (((0,), (0,)), ((), ()))
    sf_ref[...] = jnp.zeros_like(sf_ref)
    sb_ref[...] = jnp.zeros_like(sb_ref)

    def chunk_local(r0, z_ref, lb_ref, d):
        rows = pl.ds(r0, c)
        z = z_ref[rows, :]
        lb = lb_ref[...]
        e = jnp.exp(-jnp.abs(z))
        r = 1.0 / (1.0 + e)
        er = e * r
        pos = z >= 0.0
        f = lb + (1.0 - lb) * jnp.where(pos, r, er)
        kk = (1.0 - lb) * jnp.where(pos, er, r)
        g = jnp.log(f)
        ex_all = jnp.dot(expo_ref[d], _split3(g), preferred_element_type=jnp.float32)
        b = ex_all[n_levels * c:]
        ex_all = jnp.exp(ex_all)
        qf = q_ref[rows, :].astype(jnp.float32)
        v = v_ref[rows, :]
        att = mask_ref[d, n_levels] * lax.dot_general(
            qf.astype(MXU_DTYPE), kk.astype(MXU_DTYPE), nt,
            preferred_element_type=jnp.float32)
        for li in range(n_levels):
            ex = ex_all[li * c:(li + 1) * c]
            att += mask_ref[d, li] * lax.dot_general(
                (qf * ex).astype(MXU_DTYPE), (kk * ex).astype(MXU_DTYPE), nt,
                preferred_element_type=jnp.float32)
        intra = jnp.dot(att.astype(MXU_DTYPE), v, preferred_element_type=jnp.float32)
        q_dec = (qf * ex_all[n_levels * c:]).astype(MXU_DTYPE)
        b_last = b[c - 1:c] if d == 0 else b[0:1]
        kd = (kk * jnp.exp(b_last - b)).astype(MXU_DTYPE)
        update = lax.dot_general(v, kd, tn, preferred_element_type=jnp.float32)
        return rows, intra, q_dec, jnp.exp(b_last), update

    def direction(starts, z_ref, lb_ref, st_ref, out_ref, d):
        local = [chunk_local(r0, z_ref, lb_ref, d) for r0 in starts]
        st = st_ref[...]
        for rows, intra, q_dec, decay, update in local:
            out_ref[rows, :] = intra + lax.dot_general(
                q_dec, st.astype(MXU_DTYPE), nt, preferred_element_type=jnp.float32)
            st = decay * st + update
        st_ref[...] = st

    def body(n, _):
        fwd = [pl.multiple_of((n * group + i) * c, c) for i in range(group)]
        bwd = [pl.multiple_of((n_chunks - 1 - n * group - i) * c, c) for i in range(group)]
        direction(fwd, zf_ref, lbf_ref, sf_ref, of_ref, 0)
        direction(bwd, zb_ref, lbb_ref, sb_ref, ob_ref, 1)
        return 0

    lax.fori_loop(0, n_chunks // group, body, 0)

    def finish(i, _):
        rows = pl.ds(pl.multiple_of(i * out_rows, out_rows), out_rows)
        o = _rms(of_ref[rows, :] + ob_ref[rows, :], gn_ref[...])
        gate = gate_ref[rows, :].astype(jnp.float32)
        o_ref[rows, :] = (o * (gate / (1.0 + jnp.exp(-gate)))).astype(o_ref.dtype)
        return 0

    lax.fori_loop(0, t // out_rows, finish, 0)


def _hgrn(proj, zgates, lb, gn):
    b, t, _ = proj.shape
    h = N_HEADS
    c = _tile(t, HG_CHUNK)
    expo, mask, n_levels = _hgrn_constants(c)
    expo = jnp.asarray(expo, MXU_DTYPE)
    mask = jnp.asarray(mask, jnp.float32)
    out_rows = _tile(t, 512)
    group = HG_GROUP if (t // c) % HG_GROUP == 0 else 1
    head_spec = lambda off: pl.BlockSpec((None, t, HEAD_W), lambda bi, hi: (bi, 0, off + hi))
    vec_spec = pl.BlockSpec((None, 1, HEAD_W), lambda bi, hi: (hi, 0, 0))
    return pl.pallas_call(
        functools.partial(_hgrn_kernel, chunk=c, group=group, n_levels=n_levels,
                          out_rows=out_rows),
        grid=(b, h),
        in_specs=[head_spec(3 * h), head_spec(4 * h), head_spec(5 * h),
                  head_spec(0), head_spec(h), vec_spec, vec_spec, vec_spec,
                  pl.BlockSpec(expo.shape, lambda bi, hi: (0, 0, 0)),
                  pl.BlockSpec(mask.shape, lambda bi, hi: (0, 0, 0, 0))],
        out_specs=pl.BlockSpec((None, t, HEAD_W), lambda bi, hi: (bi, 0, hi)),
        out_shape=jax.ShapeDtypeStruct((b, t, h * HEAD_W), MXU_DTYPE),
        scratch_shapes=[pltpu.VMEM((t, HEAD_W), jnp.float32),
                        pltpu.VMEM((t, HEAD_W), jnp.float32),
                        pltpu.VMEM((HEAD_W, HEAD_W), jnp.float32),
                        pltpu.VMEM((HEAD_W, HEAD_W), jnp.float32)],
        compiler_params=_params("parallel", "arbitrary"),
    )(proj, proj, proj, zgates, zgates,
      lb[0].reshape(h, 1, HEAD_W), lb[1].reshape(h, 1, HEAD_W), gn.reshape(h, 1, HEAD_W),
      expo, mask)


def _conv_kernel(bg_ref, c_ref, h_ref, cp_ref, hp_ref, cn_ref, hn_ref, cw_ref, w_ref,
                 r_ref, o_ref, y_ref, u_ref, *, tm, cw, tiles_per_seq):
    i = pl.program_id(0)

    @pl.when(pl.program_id(1) == 0)
    def _():
        first = (i % tiles_per_seq) == 0
        last = (i % tiles_per_seq) == tiles_per_seq - 1
        d = c_ref.shape[1]

        def cols(n, _):
            cs = pl.ds(pl.multiple_of(n * cw, cw), cw)
            f32 = lambda ref, rows: ref[rows, cs].astype(jnp.float32)
            u = f32(c_ref, slice(None)) * f32(h_ref, slice(None))
            last_row = slice(CONV_HALO - 1, CONV_HALO)
            up = f32(cp_ref, last_row) * f32(hp_ref, last_row)
            un = f32(cn_ref, slice(0, 1)) * f32(hn_ref, slice(0, 1))
            u_ref[8:tm + 8, :] = u
            u_ref[7:8, :] = jnp.where(first, 0.0, up)
            u_ref[tm + 8:tm + 9, :] = jnp.where(last, 0.0, un)
            w = cw_ref[:, cs]
            conv = (w[0:1] * u_ref[7:tm + 7, :] + w[1:2] * u
                    + w[2:3] * u_ref[9:tm + 9, :])
            y_ref[:, cs] = (f32(bg_ref, slice(None)) * conv).astype(y_ref.dtype)
            return 0

        lax.fori_loop(0, d // cw, cols, 0)

    o_ref[...] = r_ref[...] + jnp.dot(y_ref[...], w_ref[...],
                                      preferred_element_type=jnp.float32)


def _conv_mixer(proj, conv_w, w_out, res, seq, tm=512, tn=1024, cw=256):
    m, d3 = proj.shape
    d = d3 // 3
    n = w_out.shape[1]
    tm, tn, cw = _tile(seq, tm), _tile(n, tn), _tile(d, cw)
    tiles_per_seq = seq // tm
    hb = tm // CONV_HALO
    n_hb = m // CONV_HALO
    main = lambda sec: pl.BlockSpec((tm, d), lambda i, j: (i, sec))
    prev = lambda sec: pl.BlockSpec((CONV_HALO, d),
                                    lambda i, j: (jnp.maximum(i * hb - 1, 0), sec))
    nxt = lambda sec: pl.BlockSpec((CONV_HALO, d),
                                   lambda i, j: (jnp.minimum((i + 1) * hb, n_hb - 1), sec))
    return pl.pallas_call(
        functools.partial(_conv_kernel, tm=tm, cw=cw, tiles_per_seq=tiles_per_seq),
        grid=(m // tm, n // tn),
        in_specs=[main(0), main(1), main(2), prev(1), prev(2), nxt(1), nxt(2),
                  pl.BlockSpec(conv_w.shape, lambda i, j: (0, 0)),
                  pl.BlockSpec((d, tn), lambda i, j: (0, j)),
                  pl.BlockSpec((tm, tn), lambda i, j: (i, j))],
        out_specs=pl.BlockSpec((tm, tn), lambda i, j: (i, j)),
        out_shape=jax.ShapeDtypeStruct((m, n), jnp.float32),
        scratch_shapes=[pltpu.VMEM((tm, d), MXU_DTYPE),
                        pltpu.VMEM((tm + 16, cw), jnp.float32)],
        compiler_params=_params("parallel", "arbitrary"),
    )(proj, proj, proj, proj, proj, proj, proj, conv_w, w_out, res)


def kernel(x, mix_norm_g, mlp_norm_g, final_norm_g, ab_w_in, ab_w_out, diff_lambda,
           diff_subln_g, hgrn_lb, hgrn_norm_g, conv_w_in, conv_w, conv_w_out,
           mlp_w_up, mlp_w_down):
    bsz, seq, d = x.shape
    m = bsz * seq
    h = N_HEADS
    w_attn = h * HEAD_W
    n_bf = 6 * w_attn
    cast = lambda w: w.astype(MXU_DTYPE)

    hres = x.reshape(m, d)

    w_in = ab_w_in[0]
    proj = _norm_matmul(hres, mix_norm_g[0], cast(w_in[:, :n_bf]), MXU_DTYPE)
    zgates = _norm_matmul(hres, mix_norm_g[0], cast(w_in[:, n_bf:]), jnp.float32)
    proj = proj.reshape(bsz, seq, n_bf)
    zgates = zgates.reshape(bsz, seq, 2 * w_attn)

    lam_init = 0.8 - 0.6 * math.exp(-0.3 * 0)
    slopes = 2.0 ** (-8.0 * jnp.arange(1, h + 1, dtype=jnp.float32) / h)
    slopes = jnp.broadcast_to(slopes[:, None, None], (h, 1, HEAD_W))
    oa = _attention(proj, slopes, diff_lambda[0].astype(jnp.float32), diff_subln_g[0], lam_init)

    lb = jnp.cumsum(jax.nn.softmax(hgrn_lb.astype(jnp.float32), axis=1), axis=1)[:, 0]
    ob = _hgrn(proj, zgates, lb, hgrn_norm_g[0])

    w_out = cast(ab_w_out[0])
    hres = _proj2_residual(oa.reshape(m, w_attn), ob.reshape(m, w_attn),
                           w_out[:w_attn], w_out[w_attn:], hres)
    hres = _mlp(hres, mlp_norm_g[0], cast(mlp_w_up[0]), cast(mlp_w_down[0]),
                final_norm_g, False)

    proj = _norm_matmul(hres, mix_norm_g[1], cast(conv_w_in[0]), MXU_DTYPE)
    hres = _conv_mixer(proj, conv_w[0], cast(conv_w_out[0]), hres, seq)
    hres = _mlp(hres, mlp_norm_g[1], cast(mlp_w_up[1]), cast(mlp_w_down[1]),
                final_norm_g, True)
    return hres.reshape(bsz, seq, d)
```

```python
import functools
import math

import numpy as np
import jax
import jax.numpy as jnp
from jax import lax
from jax.experimental import pallas as pl
from jax.experimental.pallas import tpu as pltpu

EPS = 1e-6
N_HEADS = 8
HEAD_W = 128
DA = 64
HG_CHUNK = 64
HG_GROUP = 4
CONV_HALO = 16
MXU_DTYPE = jnp.bfloat16
LOG2E = 1.4426950408889634
ATTN_TQ = 256
ATTN_TK = 512
ATTN_UNROLL = 4
ATTN_UNDERFLOW = 160.0
VMEM_LIMIT = 56 * 1024 * 1024


def _tile(dim, pref):
    return pref if dim % pref == 0 else dim


def _rms(x, g):
    ms = jnp.mean(x * x, axis=-1, keepdims=True)
    return x * lax.rsqrt(ms + EPS) * g


def _params(*sem):
    return pltpu.CompilerParams(dimension_semantics=sem, vmem_limit_bytes=VMEM_LIMIT)


def _norm_matmul_kernel(x_ref, g_ref, w_ref, o_ref, hn_ref):
    @pl.when(pl.program_id(1) == 0)
    def _():
        hn_ref[...] = _rms(x_ref[...], g_ref[...]).astype(hn_ref.dtype)

    o_ref[...] = jnp.dot(hn_ref[...], w_ref[...],
                         preferred_element_type=jnp.float32).astype(o_ref.dtype)


def _norm_matmul(x, g, w, out_dtype, tm=1024, tn=1024):
    m, k = x.shape
    n = w.shape[1]
    tm, tn = _tile(m, tm), _tile(n, tn)
    return pl.pallas_call(
        _norm_matmul_kernel,
        grid=(m // tm, n // tn),
        in_specs=[pl.BlockSpec((tm, k), lambda i, j: (i, 0)),
                  pl.BlockSpec((1, k), lambda i, j: (0, 0)),
                  pl.BlockSpec((k, tn), lambda i, j: (0, j))],
        out_specs=pl.BlockSpec((tm, tn), lambda i, j: (i, j)),
        out_shape=jax.ShapeDtypeStruct((m, n), out_dtype),
        scratch_shapes=[pltpu.VMEM((tm, k), MXU_DTYPE)],
        compiler_params=_params("parallel", "arbitrary"),
    )(x, g.reshape(1, k), w)


def _norm_matmul_split_kernel(x_ref, g_ref, w_ref, lo_ref, hi_ref, hn_ref, *, n_lo):
    j = pl.program_id(1)

    @pl.when(j == 0)
    def _():
        hn_ref[...] = _rms(x_ref[...], g_ref[...]).astype(hn_ref.dtype)

    r = jnp.dot(hn_ref[...], w_ref[...], preferred_element_type=jnp.float32)

    @pl.when(j < n_lo)
    def _():
        lo_ref[...] = r.astype(lo_ref.dtype)

    @pl.when(j >= n_lo)
    def _():
        hi_ref[...] = r.astype(hi_ref.dtype)


def _norm_matmul_split(x, g, w, n_split, lo_dtype, hi_dtype, tm=1024, tn=1024):
    m, k = x.shape
    n = w.shape[1]
    tm, tn = _tile(m, tm), _tile(n_split, tn)
    assert n_split % tn == 0 and (n - n_split) % tn == 0
    n_lo = n_split // tn
    return pl.pallas_call(
        functools.partial(_norm_matmul_split_kernel, n_lo=n_lo),
        grid=(m // tm, n // tn),
        in_specs=[pl.BlockSpec((tm, k), lambda i, j: (i, 0)),
                  pl.BlockSpec((1, k), lambda i, j: (0, 0)),
                  pl.BlockSpec((k, tn), lambda i, j: (0, j))],
        out_specs=[pl.BlockSpec((tm, tn), lambda i, j: (i, jnp.minimum(j, n_lo - 1))),
                   pl.BlockSpec((tm, tn), lambda i, j: (i, jnp.maximum(j - n_lo, 0)))],
        out_shape=[jax.ShapeDtypeStruct((m, n_split), lo_dtype),
                   jax.ShapeDtypeStruct((m, n - n_split), hi_dtype)],
        scratch_shapes=[pltpu.VMEM((tm, k), MXU_DTYPE)],
        compiler_params=_params("parallel", "arbitrary"),
    )(x, g.reshape(1, k), w)


def _proj2_kernel(a_ref, b_ref, wa_ref, wb_ref, r_ref, o_ref):
    acc = jnp.dot(a_ref[...], wa_ref[...], preferred_element_type=jnp.float32)
    acc += jnp.dot(b_ref[...], wb_ref[...], preferred_element_type=jnp.float32)
    o_ref[...] = r_ref[...] + acc


def _proj2_residual(a, b, wa, wb, res, tm=512, tn=2048):
    m, ka = a.shape
    kb = b.shape[1]
    n = wa.shape[1]
    tm, tn = _tile(m, tm), _tile(n, tn)
    return pl.pallas_call(
        _proj2_kernel,
        grid=(m // tm, n // tn),
        in_specs=[pl.BlockSpec((tm, ka), lambda i, j: (i, 0)),
                  pl.BlockSpec((tm, kb), lambda i, j: (i, 0)),
                  pl.BlockSpec((ka, tn), lambda i, j: (0, j)),
                  pl.BlockSpec((kb, tn), lambda i, j: (0, j)),
                  pl.BlockSpec((tm, tn), lambda i, j: (i, j))],
        out_specs=pl.BlockSpec((tm, tn), lambda i, j: (i, j)),
        out_shape=jax.ShapeDtypeStruct((m, n), jnp.float32),
        compiler_params=_params("parallel", "arbitrary"),
    )(a, b, wa, wb, res)


def _mlp_kernel(h_ref, g_ref, wu_ref, wd_ref, gf_ref, o_ref, hn_ref, *, final_norm):
    j = pl.program_id(1)

    @pl.when(j == 0)
    def _():
        h = h_ref[...]
        hn_ref[...] = _rms(h, g_ref[...]).astype(hn_ref.dtype)
        o_ref[...] = h

    a = jnp.dot(hn_ref[...], wu_ref[...], preferred_element_type=jnp.float32)
    a = jnp.maximum(a, 0.0)
    a = (a * a).astype(wd_ref.dtype)
    o_ref[...] += jnp.dot(a, wd_ref[...], preferred_element_type=jnp.float32)

    if final_norm:
        @pl.when(j == pl.num_programs(1) - 1)
        def _():
            o_ref[...] = _rms(o_ref[...], gf_ref[...])


def _mlp(h, g, w_up, w_down, g_final, final_norm, tm=1024, tf=512):
    m, d = h.shape
    f = w_up.shape[1]
    tm, tf = _tile(m, tm), _tile(f, tf)
    return pl.pallas_call(
        functools.partial(_mlp_kernel, final_norm=final_norm),
        grid=(m // tm, f // tf),
        in_specs=[pl.BlockSpec((tm, d), lambda i, j: (i, 0)),
                  pl.BlockSpec((1, d), lambda i, j: (0, 0)),
                  pl.BlockSpec((d, tf), lambda i, j: (0, j)),
                  pl.BlockSpec((tf, d), lambda i, j: (j, 0)),
                  pl.BlockSpec((1, d), lambda i, j: (0, 0))],
        out_specs=pl.BlockSpec((tm, d), lambda i, j: (i, 0)),
        out_shape=jax.ShapeDtypeStruct((m, d), jnp.float32),
        scratch_shapes=[pltpu.VMEM((tm, d), MXU_DTYPE)],
        compiler_params=_params("parallel", "arbitrary"),
    )(h, g.reshape(1, d), w_up, w_down, g_final.reshape(1, d))


def _attn_kernel(q_ref, qall_ref, k_ref, v_ref, slope_ref, lam_ref, g_ref, o_ref,
                 s0_ref, s1_ref, m_ref, mfin0_ref, mfin1_ref, acc_ref, bias_ref, vaug_ref,
                 qq_ref, win_ref, *, tq, tk, unroll, lam_init):
    t = k_ref.shape[0]
    nk, nq = t // tk, t // tq
    rows = 2 * tq
    bi, qi = pl.program_id(1), pl.program_id(2)
    q0 = qi * tq
    nt = (((1,), (1,)), ((), ()))
    c = slope_ref[...][:, :1] * LOG2E
    s_refs, mfin_refs = (s0_ref, s1_ref), (mfin0_ref, mfin1_ref)

    @pl.when((bi == 0) & (qi == 0))
    def _():
        r = lax.broadcasted_iota(jnp.int32, (tq, tk), 0)
        u = lax.broadcasted_iota(jnp.int32, (tq, tk), 1)
        rel = (r - u + t).astype(jnp.float32)

        def fill(a, _):
            u0 = pl.multiple_of(a * tk, tk)
            bias_ref[:, pl.ds(u0, tk)] = -c * jnp.abs(rel - u0.astype(jnp.float32))
            return 0

        lax.fori_loop(0, 2 * t // tk, fill, 0)

    @pl.when(qi == 0)
    def _():
        vaug_ref[:, :HEAD_W] = v_ref[...]
        vaug_ref[:, HEAD_W:] = jnp.ones((t, HEAD_W), vaug_ref.dtype)

        def max_sq_norm(ref):
            x = ref[...].astype(jnp.float32)
            return jnp.max(jnp.sum(x * x, axis=1, keepdims=True), axis=0, keepdims=True)

        bound = jnp.sqrt(max_sq_norm(qall_ref) * max_sq_norm(k_ref)) * (
            DA ** -0.5 * LOG2E * 1.01)
        dist = jnp.minimum((2.0 * bound + ATTN_UNDERFLOW) / c, 2.0 * t)
        n_win = jnp.floor((2.0 * dist + (tq - 1)) / tk) + 2.0
        n_win = jnp.minimum(jnp.ceil(n_win / unroll) * unroll, 1.0 * nk)
        win_ref[0] = n_win[0, 0].astype(jnp.int32)
        win_ref[1] = jnp.floor(dist)[0, 0].astype(jnp.int32)

    n_win, dist = win_ref[0], win_ref[1]

    def first_tile(start):
        return jnp.clip(jnp.maximum(start - dist, 0) // tk, 0, nk - n_win)

    lo_cur = first_tile(q0)
    lo_prev = first_tile(q0 - tq)

    def sweep1(cur, r):
        k0 = pl.multiple_of((lo_cur + r) * tk, tk)
        s = lax.dot_general(qq_ref[...], k_ref[pl.ds(k0, tk), :], nt,
                            preferred_element_type=jnp.float32)
        bias = bias_ref[:, pl.ds(pl.multiple_of(k0 - q0 + t, HEAD_W), tk)]
        st = s + jnp.concatenate([bias, bias], axis=0)
        s_refs[cur][:, pl.ds(pl.multiple_of(r * tk, tk), tk)] = st
        part = functools.reduce(
            jnp.maximum, [st[:, a * HEAD_W:(a + 1) * HEAD_W] for a in range(tk // HEAD_W)])
        m_ref[...] = jnp.maximum(m_ref[...], part)

    def sweep2(prev, r):
        k0 = pl.multiple_of((lo_prev + r) * tk, tk)
        cols = pl.ds(pl.multiple_of(r * tk, tk), tk)
        for mp in range(2):
            rs = slice(mp * tq, (mp + 1) * tq)
            m_rep = mfin_refs[prev][rs, :]
            p = jnp.exp2(s_refs[prev][rs, cols] - jnp.concatenate([m_rep] * (tk // HEAD_W), axis=1))
            acc_ref[mp] += jnp.dot(p.astype(vaug_ref.dtype), vaug_ref[pl.ds(k0, tk), :],
                                   preferred_element_type=jnp.float32)

    def start_sweep1():
        q = q_ref[...].astype(jnp.float32) * (DA ** -0.5 * LOG2E)
        lane = lax.broadcasted_iota(jnp.int32, q.shape, 1)
        qq_ref[...] = jnp.concatenate([jnp.where(lane < DA, q, 0.0),
                                       jnp.where(lane >= DA, q, 0.0)],
                                      axis=0).astype(qq_ref.dtype)
        m_ref[...] = jnp.full(m_ref.shape, -jnp.inf, jnp.float32)

    def end_sweep1(cur):
        m = jnp.max(m_ref[...], axis=1, keepdims=True)
        mfin_refs[cur][...] = jnp.broadcast_to(m, (rows, HEAD_W))

    def start_sweep2():
        acc_ref[...] = jnp.zeros(acc_ref.shape, jnp.float32)

    def end_sweep2():
        lp = lam_ref[...]
        lam = (jnp.exp(jnp.sum(lp[0:1] * lp[1:2], axis=1, keepdims=True))
               - jnp.exp(jnp.sum(lp[2:3] * lp[3:4], axis=1, keepdims=True)) + lam_init)
        o = [acc_ref[mp][:, :HEAD_W] / acc_ref[mp][:, HEAD_W:HEAD_W + 1] for mp in range(2)]
        o = o[0] - lam * o[1]
        o_ref[...] = (_rms(o, g_ref[...]) * (1.0 - lam_init)).astype(o_ref.dtype)

    def loop(*parts):
        def body(i, _):
            for part in parts:
                for u in range(unroll):
                    part(i * unroll + u)
            return 0
        lax.fori_loop(0, n_win // unroll, body, 0)

    @pl.when(qi == 0)
    def _():
        start_sweep1()
        loop(functools.partial(sweep1, 0))
        end_sweep1(0)

    for cur in range(2):
        @pl.when((qi > 0) & (qi < nq) & (qi % 2 == cur))
        def _():
            start_sweep1()
            start_sweep2()
            loop(functools.partial(sweep1, cur), functools.partial(sweep2, 1 - cur))
            end_sweep1(cur)
            end_sweep2()

    @pl.when(qi == nq)
    def _():
        start_sweep2()
        loop(functools.partial(sweep2, (nq - 1) % 2))
        end_sweep2()


def _attention(proj, slopes, lam_p, subln_g, lam_init):
    b, t, _ = proj.shape
    tq, tk = _tile(t, ATTN_TQ), _tile(t, ATTN_TK)
    nq = t // tq
    h = N_HEADS
    return pl.pallas_call(
        functools.partial(_attn_kernel, tq=tq, tk=tk, lam_init=lam_init,
                          unroll=ATTN_UNROLL if (t // tk) % ATTN_UNROLL == 0 else 1),
        grid=(h, b, nq + 1),
        in_specs=[pl.BlockSpec((None, tq, HEAD_W),
                               lambda hi, bi, qi: (bi, jnp.minimum(qi, nq - 1), hi)),
                  pl.BlockSpec((None, t, HEAD_W), lambda hi, bi, qi: (bi, 0, hi)),
                  pl.BlockSpec((None, t, HEAD_W), lambda hi, bi, qi: (bi, 0, h + hi)),
                  pl.BlockSpec((None, t, HEAD_W), lambda hi, bi, qi: (bi, 0, 2 * h + hi)),
                  pl.BlockSpec((None, 1, HEAD_W), lambda hi, bi, qi: (hi, 0, 0)),
                  pl.BlockSpec(lam_p.shape, lambda hi, bi, qi: (0, 0)),
                  pl.BlockSpec((1, HEAD_W), lambda hi, bi, qi: (0, 0))],
        out_specs=pl.BlockSpec((None, tq, HEAD_W),
                               lambda hi, bi, qi: (bi, jnp.maximum(qi - 1, 0), hi)),
        out_shape=jax.ShapeDtypeStruct((b, t, h * HEAD_W), MXU_DTYPE),
        scratch_shapes=[pltpu.VMEM((2 * tq, t), jnp.float32),
                        pltpu.VMEM((2 * tq, t), jnp.float32),
                        pltpu.VMEM((2 * tq, HEAD_W), jnp.float32),
                        pltpu.VMEM((2 * tq, HEAD_W), jnp.float32),
                        pltpu.VMEM((2 * tq, HEAD_W), jnp.float32),
                        pltpu.VMEM((2, tq, 2 * HEAD_W), jnp.float32),
                        pltpu.VMEM((tq, 2 * t), jnp.float32),
                        pltpu.VMEM((t, 2 * HEAD_W), MXU_DTYPE),
                        pltpu.VMEM((2 * tq, HEAD_W), MXU_DTYPE),
                        pltpu.SMEM((2,), jnp.int32)],
        compiler_params=_params("parallel", "arbitrary", "arbitrary"),
    )(proj, proj, proj, proj, slopes, lam_p, subln_g.reshape(1, HEAD_W))


def _hgrn_constants(c):
    levels = []
    size = 2
    while size <= c:
        levels.append(size)
        size *= 2
    expo = np.zeros((len(levels) + 1, c, c), np.float32)
    mask = np.zeros((len(levels) + 1, c, c), np.float32)
    for li, size in enumerate(levels):
        half = size // 2
        for t in range(c):
            pos = t % size
            bd = t - pos + half - 1
            if pos >= half:
                expo[li, t, bd + 1:t + 1] = 1.0
                mask[li, t, t - pos:t - pos + half] = 1.0
            else:
                expo[li, t, t + 1:bd + 1] = 1.0
    expo[-1] = np.tril(np.ones((c, c), np.float32))
    mask[-1] = np.eye(c, dtype=np.float32)

    def pack(e):
        e = e.reshape(-1, c)
        return np.concatenate([e, e, e], axis=1)

    return (np.stack([pack(expo), pack(expo[:, ::-1, ::-1])]),
            np.stack([mask, mask[:, ::-1, ::-1]]), len(levels))


def _split3(x):
    hi = x.astype(jnp.bfloat16)
    r1 = x - hi.astype(jnp.float32)
    mid = r1.astype(jnp.bfloat16)
    lo = (r1 - mid.astype(jnp.float32)).astype(jnp.bfloat16)
    return jnp.concatenate([hi, mid, lo], axis=0)


def _hgrn_kernel(q_ref, v_ref, gate_ref, zf_ref, zb_ref, lbf_ref, lbb_ref, gn_ref,
                 expo_ref, mask_ref, o_ref, of_ref, ob_ref, sf_ref, sb_ref,
                 *, chunk, group, n_levels, out_rows):
    t = q_ref.shape[0]
    c = chunk
    n_chunks = t // c
    nt = (((1,), (1,)), ((), ()))
    tn = (((0,), (0,)), ((), ()))
    sf_ref[...] = jnp.zeros_like(sf_ref)
    sb_ref[...] = jnp.zeros_like(sb_ref)

    def chunk_local(r0, z_ref, lb_ref, d):
        rows = pl.ds(r0, c)
        z = z_ref[rows, :]
        lb = lb_ref[...]
        e = jnp.exp(-jnp.abs(z))
        r = 1.0 / (1.0 + e)
        er = e * r
        pos = z >= 0.0
        f = lb + (1.0 - lb) * jnp.where(pos, r, er)
        kk = (1.0 - lb) * jnp.where(pos, er, r)
        g = jnp.log(f)
        ex_all = jnp.dot(expo_ref[d], _split3(g), preferred_element_type=jnp.float32)
        b = ex_all[n_levels * c:]
        ex_all = jnp.exp(ex_all)
        qf = q_ref[rows, :].astype(jnp.float32)
        v = v_ref[rows, :]
        att = mask_ref[d, n_levels] * lax.dot_general(
            qf.astype(MXU_DTYPE), kk.astype(MXU_DTYPE), nt,
            preferred_element_type=jnp.float32)
        for li in range(n_levels):
            ex = ex_all[li * c:(li + 1) * c]
            att += mask_ref[d, li] * lax.dot_general(
                (qf * ex).astype(MXU_DTYPE), (kk * ex).astype(MXU_DTYPE), nt,
                preferred_element_type=jnp.float32)
        intra = jnp.dot(att.astype(MXU_DTYPE), v, preferred_element_type=jnp.float32)
        q_dec = (qf * ex_all[n_levels * c:]).astype(MXU_DTYPE)
        b_last = b[c - 1:c] if d == 0 else b[0:1]
        kd = (kk * jnp.exp(b_last - b)).astype(MXU_DTYPE)
        update = lax.dot_general(v, kd, tn, preferred_element_type=jnp.float32)
        return rows, intra, q_dec, jnp.exp(b_last), update

    def direction(starts, z_ref, lb_ref, st_ref, out_ref, d):
        local = [chunk_local(r0, z_ref, lb_ref, d) for r0 in starts]
        st = st_ref[...]
        for rows, intra, q_dec, decay, update in local:
            out_ref[rows, :] = intra + lax.dot_general(
                q_dec, st.astype(MXU_DTYPE), nt, preferred_element_type=jnp.float32)
            st = decay * st + update
        st_ref[...] = st

    def body(n, _):
        fwd = [pl.multiple_of((n * group + i) * c, c) for i in range(group)]
        bwd = [pl.multiple_of((n_chunks - 1 - n * group - i) * c, c) for i in range(group)]
        direction(fwd, zf_ref, lbf_ref, sf_ref, of_ref, 0)
        direction(bwd, zb_ref, lbb_ref, sb_ref, ob_ref, 1)
        return 0

    lax.fori_loop(0, n_chunks // group, body, 0)

    def finish(i, _):
        rows = pl.ds(pl.multiple_of(i * out_rows, out_rows), out_rows)
        o = _rms(of_ref[rows, :] + ob_ref[rows, :], gn_ref[...])
        gate = gate_ref[rows, :].astype(jnp.float32)
        o_ref[rows, :] = (o * (gate / (1.0 + jnp.exp(-gate)))).astype(o_ref.dtype)
        return 0

    lax.fori_loop(0, t // out_rows, finish, 0)


def _hgrn(proj, zgates, lb, gn):
    b, t, _ = proj.shape
    h = N_HEADS
    c = _tile(t, HG_CHUNK)
    expo, mask, n_levels = _hgrn_constants(c)
    expo = jnp.asarray(expo, MXU_DTYPE)
    mask = jnp.asarray(mask, jnp.float32)
    out_rows = _tile(t, 512)
    group = HG_GROUP if (t // c) % HG_GROUP == 0 else 1
    head_spec = lambda off: pl.BlockSpec((None, t, HEAD_W), lambda bi, hi: (bi, 0, off + hi))
    vec_spec = pl.BlockSpec((None, 1, HEAD_W), lambda bi, hi: (hi, 0, 0))
    return pl.pallas_call(
        functools.partial(_hgrn_kernel, chunk=c, group=group, n_levels=n_levels,
                          out_rows=out_rows),
        grid=(b, h),
        in_specs=[head_spec(3 * h), head_spec(4 * h), head_spec(5 * h),
                  head_spec(0), head_spec(h), vec_spec, vec_spec, vec_spec,
                  pl.BlockSpec(expo.shape, lambda bi, hi: (0, 0, 0)),
                  pl.BlockSpec(mask.shape, lambda bi, hi: (0, 0, 0, 0))],
        out_specs=pl.BlockSpec((None, t, HEAD_W), lambda bi, hi: (bi, 0, hi)),
        out_shape=jax.ShapeDtypeStruct((b, t, h * HEAD_W), MXU_DTYPE),
        scratch_shapes=[pltpu.VMEM((t, HEAD_W), jnp.float32),
                        pltpu.VMEM((t, HEAD_W), jnp.float32),
                        pltpu.VMEM((HEAD_W, HEAD_W), jnp.float32),
                        pltpu.VMEM((HEAD_W, HEAD_W), jnp.float32)],
        compiler_params=_params("parallel", "arbitrary"),
    )(proj, proj, proj, zgates, zgates,
      lb[0].reshape(h, 1, HEAD_W), lb[1].reshape(h, 1, HEAD_W), gn.reshape(h, 1, HEAD_W),
      expo, mask)


def _conv_kernel(bg_ref, c_ref, h_ref, cp_ref, hp_ref, cn_ref, hn_ref, cw_ref, w_ref,
                 r_ref, o_ref, y_ref, u_ref, *, tm, cw, tiles_per_seq):
    i = pl.program_id(0)

    @pl.when(pl.program_id(1) == 0)
    def _():
        first = (i % tiles_per_seq) == 0
        last = (i % tiles_per_seq) == tiles_per_seq - 1
        d = c_ref.shape[1]

        def cols(n, _):
            cs = pl.ds(pl.multiple_of(n * cw, cw), cw)
            f32 = lambda ref, rows: ref[rows, cs].astype(jnp.float32)
            u = f32(c_ref, slice(None)) * f32(h_ref, slice(None))
            last_row = slice(CONV_HALO - 1, CONV_HALO)
            up = f32(cp_ref, last_row) * f32(hp_ref, last_row)
            un = f32(cn_ref, slice(0, 1)) * f32(hn_ref, slice(0, 1))
            u_ref[8:tm + 8, :] = u
            u_ref[7:8, :] = jnp.where(first, 0.0, up)
            u_ref[tm + 8:tm + 9, :] = jnp.where(last, 0.0, un)
            w = cw_ref[:, cs]
            conv = (w[0:1] * u_ref[7:tm + 7, :] + w[1:2] * u
                    + w[2:3] * u_ref[9:tm + 9, :])
            y_ref[:, cs] = (f32(bg_ref, slice(None)) * conv).astype(y_ref.dtype)
            return 0

        lax.fori_loop(0, d // cw, cols, 0)

    o_ref[...] = r_ref[...] + jnp.dot(y_ref[...], w_ref[...],
                                      preferred_element_type=jnp.float32)


def _conv_mixer(proj, conv_w, w_out, res, seq, tm=512, tn=2048, cw=256):
    m, d3 = proj.shape
    d = d3 // 3
    n = w_out.shape[1]
    tm, tn, cw = _tile(seq, tm), _tile(n, tn), _tile(d, cw)
    tiles_per_seq = seq // tm
    hb = tm // CONV_HALO
    n_hb = m // CONV_HALO
    main = lambda sec: pl.BlockSpec((tm, d), lambda i, j: (i, sec))
    prev = lambda sec: pl.BlockSpec((CONV_HALO, d),
                                    lambda i, j: (jnp.maximum(i * hb - 1, 0), sec))
    nxt = lambda sec: pl.BlockSpec((CONV_HALO, d),
                                   lambda i, j: (jnp.minimum((i + 1) * hb, n_hb - 1), sec))
    return pl.pallas_call(
        functools.partial(_conv_kernel, tm=tm, cw=cw, tiles_per_seq=tiles_per_seq),
        grid=(m // tm, n // tn),
        in_specs=[main(0), main(1), main(2), prev(1), prev(2), nxt(1), nxt(2),
                  pl.BlockSpec(conv_w.shape, lambda i, j: (0, 0)),
                  pl.BlockSpec((d, tn), lambda i, j: (0, j)),
                  pl.BlockSpec((tm, tn), lambda i, j: (i, j))],
        out_specs=pl.BlockSpec((tm, tn), lambda i, j: (i, j)),
        out_shape=jax.ShapeDtypeStruct((m, n), jnp.float32),
        scratch_shapes=[pltpu.VMEM((tm, d), MXU_DTYPE),
                        pltpu.VMEM((tm + 16, cw), jnp.float32)],
        compiler_params=_params("parallel", "arbitrary"),
    )(proj, proj, proj, proj, proj, proj, proj, conv_w, w_out, res)


def kernel(x, mix_norm_g, mlp_norm_g, final_norm_g, ab_w_in, ab_w_out, diff_lambda,
           diff_subln_g, hgrn_lb, hgrn_norm_g, conv_w_in, conv_w, conv_w_out,
           mlp_w_up, mlp_w_down):
    bsz, seq, d = x.shape
    m = bsz * seq
    h = N_HEADS
    w_attn = h * HEAD_W
    n_bf = 6 * w_attn
    cast = lambda w: w.astype(MXU_DTYPE)

    hres = x.reshape(m, d)

    proj, zgates = _norm_matmul_split(hres, mix_norm_g[0], cast(ab_w_in[0]), n_bf,
                                      MXU_DTYPE, jnp.float32)
    proj = proj.reshape(bsz, seq, n_bf)
    zgates = zgates.reshape(bsz, seq, 2 * w_attn)

    lam_init = 0.8 - 0.6 * math.exp(-0.3 * 0)
    slopes = 2.0 ** (-8.0 * jnp.arange(1, h + 1, dtype=jnp.float32) / h)
    slopes = jnp.broadcast_to(slopes[:, None, None], (h, 1, HEAD_W))
    oa = _attention(proj, slopes, diff_lambda[0].astype(jnp.float32), diff_subln_g[0], lam_init)

    lb = jnp.cumsum(jax.nn.softmax(hgrn_lb.astype(jnp.float32), axis=1), axis=1)[:, 0]
    ob = _hgrn(proj, zgates, lb, hgrn_norm_g[0])

    w_out = cast(ab_w_out[0])
    hres = _proj2_residual(oa.reshape(m, w_attn), ob.reshape(m, w_attn),
                           w_out[:w_attn], w_out[w_attn:], hres)
    hres = _mlp(hres, mlp_norm_g[0], cast(mlp_w_up[0]), cast(mlp_w_down[0]),
                final_norm_g, False)

    proj = _norm_matmul(hres, mix_norm_g[1], cast(conv_w_in[0]), MXU_DTYPE)
    hres = _conv_mixer(proj, conv_w[0], cast(conv_w_out[0]), hres, seq)
    hres = _mlp(hres, mlp_norm_g[1], cast(mlp_w_up[1]), cast(mlp_w_down[1]),
                final_norm_g, True)
    return hres.reshape(bsz, seq, d)
```

```python
import functools
import math

import numpy as np
import jax
import jax.numpy as jnp
from jax import lax
from jax.experimental import pallas as pl
from jax.experimental.pallas import tpu as pltpu

EPS = 1e-6
N_HEADS = 8
HEAD_W = 128
DA = 64
HG_CHUNK = 64
HG_GROUP = 8
CONV_HALO = 16
MXU_DTYPE = jnp.bfloat16
LOG2E = 1.4426950408889634
ATTN_TQ = 256
ATTN_TK = 512
ATTN_UNROLL = 4
ATTN_UNDERFLOW = 160.0
VMEM_LIMIT = 56 * 1024 * 1024


def _tile(dim, pref):
    return pref if dim % pref == 0 else dim


def _rms(x, g):
    ms = jnp.mean(x * x, axis=-1, keepdims=True)
    return x * lax.rsqrt(ms + EPS) * g


def _params(*sem):
    return pltpu.CompilerParams(dimension_semantics=sem, vmem_limit_bytes=VMEM_LIMIT)


def _norm_matmul_kernel(x_ref, g_ref, w_ref, o_ref, hn_ref):
    @pl.when(pl.program_id(1) == 0)
    def _():
        hn_ref[...] = _rms(x_ref[...], g_ref[...]).astype(hn_ref.dtype)

    o_ref[...] = jnp.dot(hn_ref[...], w_ref[...],
                         preferred_element_type=jnp.float32).astype(o_ref.dtype)


def _norm_matmul(x, g, w, out_dtype, tm=1024, tn=1024):
    m, k = x.shape
    n = w.shape[1]
    tm, tn = _tile(m, tm), _tile(n, tn)
    return pl.pallas_call(
        _norm_matmul_kernel,
        grid=(m // tm, n // tn),
        in_specs=[pl.BlockSpec((tm, k), lambda i, j: (i, 0)),
                  pl.BlockSpec((1, k), lambda i, j: (0, 0)),
                  pl.BlockSpec((k, tn), lambda i, j: (0, j))],
        out_specs=pl.BlockSpec((tm, tn), lambda i, j: (i, j)),
        out_shape=jax.ShapeDtypeStruct((m, n), out_dtype),
        scratch_shapes=[pltpu.VMEM((tm, k), MXU_DTYPE)],
        compiler_params=_params("parallel", "arbitrary"),
    )(x, g.reshape(1, k), w)


def _norm_matmul_split_kernel(x_ref, g_ref, w_ref, lo_ref, hi_ref, hn_ref, *, n_lo):
    j = pl.program_id(1)

    @pl.when(j == 0)
    def _():
        hn_ref[...] = _rms(x_ref[...], g_ref[...]).astype(hn_ref.dtype)

    r = jnp.dot(hn_ref[...], w_ref[...], preferred_element_type=jnp.float32)

    @pl.when(j < n_lo)
    def _():
        lo_ref[...] = r.astype(lo_ref.dtype)

    @pl.when(j >= n_lo)
    def _():
        hi_ref[...] = r.astype(hi_ref.dtype)


def _norm_matmul_split(x, g, w, n_split, lo_dtype, hi_dtype, tm=1024, tn=1024):
    m, k = x.shape
    n = w.shape[1]
    tm, tn = _tile(m, tm), _tile(n_split, tn)
    assert n_split % tn == 0 and (n - n_split) % tn == 0
    n_lo = n_split // tn
    return pl.pallas_call(
        functools.partial(_norm_matmul_split_kernel, n_lo=n_lo),
        grid=(m // tm, n // tn),
        in_specs=[pl.BlockSpec((tm, k), lambda i, j: (i, 0)),
                  pl.BlockSpec((1, k), lambda i, j: (0, 0)),
                  pl.BlockSpec((k, tn), lambda i, j: (0, j))],
        out_specs=[pl.BlockSpec((tm, tn), lambda i, j: (i, jnp.minimum(j, n_lo - 1))),
                   pl.BlockSpec((tm, tn), lambda i, j: (i, jnp.maximum(j - n_lo, 0)))],
        out_shape=[jax.ShapeDtypeStruct((m, n_split), lo_dtype),
                   jax.ShapeDtypeStruct((m, n - n_split), hi_dtype)],
        scratch_shapes=[pltpu.VMEM((tm, k), MXU_DTYPE)],
        compiler_params=_params("parallel", "arbitrary"),
    )(x, g.reshape(1, k), w)


def _proj2_kernel(a_ref, b_ref, wa_ref, wb_ref, r_ref, o_ref):
    acc = jnp.dot(a_ref[...], wa_ref[...], preferred_element_type=jnp.float32)
    acc += jnp.dot(b_ref[...], wb_ref[...], preferred_element_type=jnp.float32)
    o_ref[...] = r_ref[...] + acc


def _proj2_residual(a, b, w, res, tm=512, tn=2048):
    m, ka = a.shape
    kb = b.shape[1]
    n = w.shape[1]
    assert ka == kb and w.shape[0] == ka + kb
    tm, tn = _tile(m, tm), _tile(n, tn)
    return pl.pallas_call(
        _proj2_kernel,
        grid=(m // tm, n // tn),
        in_specs=[pl.BlockSpec((tm, ka), lambda i, j: (i, 0)),
                  pl.BlockSpec((tm, kb), lambda i, j: (i, 0)),
                  pl.BlockSpec((ka, tn), lambda i, j: (0, j)),
                  pl.BlockSpec((kb, tn), lambda i, j: (1, j)),
                  pl.BlockSpec((tm, tn), lambda i, j: (i, j))],
        out_specs=pl.BlockSpec((tm, tn), lambda i, j: (i, j)),
        out_shape=jax.ShapeDtypeStruct((m, n), jnp.float32),
        compiler_params=_params("parallel", "arbitrary"),
    )(a, b, w, w, res)


def _mlp_kernel(h_ref, g_ref, wu_ref, wd_ref, gf_ref, o_ref, hn_ref, *, final_norm):
    j = pl.program_id(1)

    @pl.when(j == 0)
    def _():
        h = h_ref[...]
        hn_ref[...] = _rms(h, g_ref[...]).astype(hn_ref.dtype)
        o_ref[...] = h

    a = jnp.dot(hn_ref[...], wu_ref[...], preferred_element_type=jnp.float32)
    a = jnp.maximum(a, 0.0)
    a = (a * a).astype(wd_ref.dtype)
    o_ref[...] += jnp.dot(a, wd_ref[...], preferred_element_type=jnp.float32)

    if final_norm:
        @pl.when(j == pl.num_programs(1) - 1)
        def _():
            o_ref[...] = _rms(o_ref[...], gf_ref[...])


def _mlp(h, g, w_up, w_down, g_final, final_norm, tm=1024, tf=512):
    m, d = h.shape
    f = w_up.shape[1]
    tm, tf = _tile(m, tm), _tile(f, tf)
    return pl.pallas_call(
        functools.partial(_mlp_kernel, final_norm=final_norm),
        grid=(m // tm, f // tf),
        in_specs=[pl.BlockSpec((tm, d), lambda i, j: (i, 0)),
                  pl.BlockSpec((1, d), lambda i, j: (0, 0)),
                  pl.BlockSpec((d, tf), lambda i, j: (0, j)),
                  pl.BlockSpec((tf, d), lambda i, j: (j, 0)),
                  pl.BlockSpec((1, d), lambda i, j: (0, 0))],
        out_specs=pl.BlockSpec((tm, d), lambda i, j: (i, 0)),
        out_shape=jax.ShapeDtypeStruct((m, d), jnp.float32),
        scratch_shapes=[pltpu.VMEM((tm, d), MXU_DTYPE)],
        compiler_params=_params("parallel", "arbitrary"),
    )(h, g.reshape(1, d), w_up, w_down, g_final.reshape(1, d))


def _attn_kernel(q_ref, qall_ref, k_ref, v_ref, slope_ref, lam_ref, g_ref, o_ref,
                 s0_ref, s1_ref, m_ref, mfin0_ref, mfin1_ref, acc_ref, bias_ref, vaug_ref,
                 qq_ref, win_ref, *, tq, tk, unroll, lam_init):
    t = k_ref.shape[0]
    nk, nq = t // tk, t // tq
    rows = 2 * tq
    bi, qi = pl.program_id(1), pl.program_id(2)
    q0 = qi * tq
    nt = (((1,), (1,)), ((), ()))
    c = slope_ref[...][:, :1] * LOG2E
    s_refs, mfin_refs = (s0_ref, s1_ref), (mfin0_ref, mfin1_ref)

    strip_rows, strip_w = bias_ref.shape
    strip_origin = t + tq - strip_rows

    @pl.when((bi == 0) & (qi == 0))
    def _():
        r = lax.broadcasted_iota(jnp.int32, (strip_rows, HEAD_W), 0)
        u = lax.broadcasted_iota(jnp.int32, (strip_rows, HEAD_W), 1)
        rel = (r - u + strip_origin).astype(jnp.float32)

        def fill(a, _):
            u0 = pl.multiple_of(a * HEAD_W, HEAD_W)
            bias_ref[:, pl.ds(u0, HEAD_W)] = -c * jnp.abs(rel - u0.astype(jnp.float32))
            return 0

        lax.fori_loop(0, strip_w // HEAD_W, fill, 0)

    @pl.when(qi == 0)
    def _():
        vaug_ref[:, :HEAD_W] = v_ref[...]
        vaug_ref[:, HEAD_W:] = jnp.ones((t, HEAD_W), vaug_ref.dtype)

        def max_sq_norm(ref):
            x = ref[...].astype(jnp.float32)
            return jnp.max(jnp.sum(x * x, axis=1, keepdims=True), axis=0, keepdims=True)

        bound = jnp.sqrt(max_sq_norm(qall_ref) * max_sq_norm(k_ref)) * (
            DA ** -0.5 * LOG2E * 1.01)
        dist = jnp.minimum((2.0 * bound + ATTN_UNDERFLOW) / c, 2.0 * t)
        n_win = jnp.floor((2.0 * dist + (tq - 1)) / tk) + 2.0
        n_win = jnp.minimum(jnp.ceil(n_win / unroll) * unroll, 1.0 * nk)
        win_ref[0] = n_win[0, 0].astype(jnp.int32)
        win_ref[1] = jnp.floor(dist)[0, 0].astype(jnp.int32)

    n_win, dist = win_ref[0], win_ref[1]

    def first_tile(start):
        return jnp.clip(jnp.maximum(start - dist, 0) // tk, 0, nk - n_win)

    lo_cur = first_tile(q0)
    lo_prev = first_tile(q0 - tq)

    def sweep1(cur, r):
        k0 = pl.multiple_of((lo_cur + r) * tk, tk)
        s = lax.dot_general(qq_ref[...], k_ref[pl.ds(k0, tk), :], nt,
                            preferred_element_type=jnp.float32)
        bias = [bias_ref[:, pl.ds(pl.multiple_of(k0 - q0 - rb * strip_rows + strip_origin,
                                                 HEAD_W), tk)]
                for rb in range(tq // strip_rows)]
        st = s + jnp.concatenate(bias + bias, axis=0)
        s_refs[cur][:, pl.ds(pl.multiple_of(r * tk, tk), tk)] = st
        part = functools.reduce(
            jnp.maximum, [st[:, a * HEAD_W:(a + 1) * HEAD_W] for a in range(tk // HEAD_W)])
        m_ref[...] = jnp.maximum(m_ref[...], part)

    def sweep2(prev, r):
        k0 = pl.multiple_of((lo_prev + r) * tk, tk)
        cols = pl.ds(pl.multiple_of(r * tk, tk), tk)
        for mp in range(2):
            rs = slice(mp * tq, (mp + 1) * tq)
            m_rep = mfin_refs[prev][rs, :]
            p = jnp.exp2(s_refs[prev][rs, cols] - jnp.concatenate([m_rep] * (tk // HEAD_W), axis=1))
            acc_ref[mp] += jnp.dot(p.astype(vaug_ref.dtype), vaug_ref[pl.ds(k0, tk), :],
                                   preferred_element_type=jnp.float32)

    def start_sweep1():
        q = q_ref[...].astype(jnp.float32) * (DA ** -0.5 * LOG2E)
        lane = lax.broadcasted_iota(jnp.int32, q.shape, 1)
        qq_ref[...] = jnp.concatenate([jnp.where(lane < DA, q, 0.0),
                                       jnp.where(lane >= DA, q, 0.0)],
                                      axis=0).astype(qq_ref.dtype)
        m_ref[...] = jnp.full(m_ref.shape, -jnp.inf, jnp.float32)

    def end_sweep1(cur):
        m = jnp.max(m_ref[...], axis=1, keepdims=True)
        mfin_refs[cur][...] = jnp.broadcast_to(m, (rows, HEAD_W))

    def start_sweep2():
        acc_ref[...] = jnp.zeros(acc_ref.shape, jnp.float32)

    def end_sweep2():
        lp = lam_ref[...]
        lam = (jnp.exp(jnp.sum(lp[0:1] * lp[1:2], axis=1, keepdims=True))
               - jnp.exp(jnp.sum(lp[2:3] * lp[3:4], axis=1, keepdims=True)) + lam_init)
        o = [acc_ref[mp][:, :HEAD_W] / acc_ref[mp][:, HEAD_W:] for mp in range(2)]
        o = o[0] - lam * o[1]
        o_ref[...] = (_rms(o, g_ref[...]) * (1.0 - lam_init)).astype(o_ref.dtype)

    def loop(*parts):
        def body(i, _):
            for part in parts:
                for u in range(unroll):
                    part(i * unroll + u)
            return 0
        lax.fori_loop(0, n_win // unroll, body, 0)

    @pl.when(qi == 0)
    def _():
        start_sweep1()
        loop(functools.partial(sweep1, 0))
        end_sweep1(0)

    for cur in range(2):
        @pl.when((qi > 0) & (qi < nq) & (qi % 2 == cur))
        def _():
            start_sweep1()
            start_sweep2()
            loop(functools.partial(sweep1, cur), functools.partial(sweep2, 1 - cur))
            end_sweep1(cur)
            end_sweep2()

    @pl.when(qi == nq)
    def _():
        start_sweep2()
        loop(functools.partial(sweep2, (nq - 1) % 2))
        end_sweep2()


def _attention(proj, slopes, lam_p, subln_g, lam_init):
    b, t, _ = proj.shape
    tq, tk = _tile(t, ATTN_TQ), _tile(t, ATTN_TK)
    nq = t // tq
    strip_rows = min(tq, HEAD_W)
    h = N_HEADS
    return pl.pallas_call(
        functools.partial(_attn_kernel, tq=tq, tk=tk, lam_init=lam_init,
                          unroll=ATTN_UNROLL if (t // tk) % ATTN_UNROLL == 0 else 1),
        grid=(h, b, nq + 1),
        in_specs=[pl.BlockSpec((None, tq, HEAD_W),
                               lambda hi, bi, qi: (bi, jnp.minimum(qi, nq - 1), hi)),
                  pl.BlockSpec((None, t, HEAD_W), lambda hi, bi, qi: (bi, 0, hi)),
                  pl.BlockSpec((None, t, HEAD_W), lambda hi, bi, qi: (bi, 0, h + hi)),
                  pl.BlockSpec((None, t, HEAD_W), lambda hi, bi, qi: (bi, 0, 2 * h + hi)),
                  pl.BlockSpec((None, 1, HEAD_W), lambda hi, bi, qi: (hi, 0, 0)),
                  pl.BlockSpec(lam_p.shape, lambda hi, bi, qi: (0, 0)),
                  pl.BlockSpec((1, HEAD_W), lambda hi, bi, qi: (0, 0))],
        out_specs=pl.BlockSpec((None, tq, HEAD_W),
                               lambda hi, bi, qi: (bi, jnp.maximum(qi - 1, 0), hi)),
        out_shape=jax.ShapeDtypeStruct((b, t, h * HEAD_W), MXU_DTYPE),
        scratch_shapes=[pltpu.VMEM((2 * tq, t), jnp.float32),
                        pltpu.VMEM((2 * tq, t), jnp.float32),
                        pltpu.VMEM((2 * tq, HEAD_W), jnp.float32),
                        pltpu.VMEM((2 * tq, HEAD_W), jnp.float32),
                        pltpu.VMEM((2 * tq, HEAD_W), jnp.float32),
                        pltpu.VMEM((2, tq, 2 * HEAD_W), jnp.float32),
                        pltpu.VMEM((strip_rows, 2 * t + tq - strip_rows), jnp.float32),
                        pltpu.VMEM((t, 2 * HEAD_W), MXU_DTYPE),
                        pltpu.VMEM((2 * tq, HEAD_W), MXU_DTYPE),
                        pltpu.SMEM((2,), jnp.int32)],
        compiler_params=_params("parallel", "arbitrary", "arbitrary"),
    )(proj, proj, proj, proj, slopes, lam_p, subln_g.reshape(1, HEAD_W))


def _hgrn_constants(c):
    levels = []
    size = 2
    while size <= c:
        levels.append(size)
        size *= 2
    expo = np.zeros((len(levels) + 1, c, c), np.float32)
    mask = np.zeros((len(levels) + 1, c, c), np.float32)
    for li, size in enumerate(levels):
        half = size // 2
        for t in range(c):
            pos = t % size
            bd = t - pos + half - 1
            if pos >= half:
                expo[li, t, bd + 1:t + 1] = 1.0
                mask[li, t, t - pos:t - pos + half] = 1.0
            else:
                expo[li, t, t + 1:bd + 1] = 1.0
    expo[-1] = np.tril(np.ones((c, c), np.float32))
    mask[-1] = np.eye(c, dtype=np.float32)

    def pack(e):
        e = e.reshape(-1, c)
        return np.concatenate([e, e, e], axis=1)

    return (np.stack([pack(expo), pack(expo[:, ::-1, ::-1])]),
            np.stack([mask, mask[:, ::-1, ::-1]]), len(levels))


def _split3(x):
    hi = x.astype(jnp.bfloat16)
    r1 = x - hi.astype(jnp.float32)
    mid = r1.astype(jnp.bfloat16)
    lo = (r1 - mid.astype(jnp.float32)).astype(jnp.bfloat16)
    return jnp.concatenate([hi, mid, lo], axis=0)


def _hgrn_kernel(q_ref, v_ref, gate_ref, zf_ref, zb_ref, lbf_ref, lbb_ref, gn_ref,
                 expo_ref, mask_ref, o_ref, of_ref, ob_ref, sf_ref, sb_ref,
                 *, chunk, group, n_levels, out_rows):
    t = q_ref.shape[0]
    c = chunk
    n_chunks = t // c
    nt = (((1,), (1,)), ((), ()))
    tn = (((0,), (0,)), ((), ()))
    sf_ref[...] = jnp.zeros_like(sf_ref)
    sb_ref[...] = jnp.zeros_like(sb_ref)

    def chunk_local(r0, z_ref, lb_ref, d):
        rows = pl.ds(r0, c)
        z = z_ref[rows, :]
        lb = lb_ref[...]
        e = jnp.exp(-jnp.abs(z))
        r = 1.0 / (1.0 + e)
        er = e * r
        pos = z >= 0.0
        f = lb + (1.0 - lb) * jnp.where(pos, r, er)
        kk = (1.0 - lb) * jnp.where(pos, er, r)
        g = jnp.log(f)
        ex_all = jnp.dot(expo_ref[d], _split3(g), preferred_element_type=jnp.float32)
        b = ex_all[n_levels * c:]
        ex_all = jnp.exp(ex_all)
        qf = q_ref[rows, :].astype(jnp.float32)
        v = v_ref[rows, :]
        att = mask_ref[d, n_levels] * lax.dot_general(
            qf.astype(MXU_DTYPE), kk.astype(MXU_DTYPE), nt,
            preferred_element_type=jnp.float32)
        for li in range(n_levels):
            ex = ex_all[li * c:(li + 1) * c]
            att += mask_ref[d, li] * lax.dot_general(
                (qf * ex).astype(MXU_DTYPE), (kk * ex).astype(MXU_DTYPE), nt,
                preferred_element_type=jnp.float32)
        intra = jnp.dot(att.astype(MXU_DTYPE), v, preferred_element_type=jnp.float32)
        q_dec = (qf * ex_all[n_levels * c:]).astype(MXU_DTYPE)
        b_last = b[c - 1:c] if d == 0 else b[0:1]
        kd = (kk * jnp.exp(b_last - b)).astype(MXU_DTYPE)
        update = lax.dot_general(v, kd, tn, preferred_element_type=jnp.float32)
        return rows, intra, q_dec, jnp.exp(b_last), update

    def direction(starts, z_ref, lb_ref, st_ref, out_ref, d):
        local = [chunk_local(r0, z_ref, lb_ref, d) for r0 in starts]
        st = st_ref[...]
        for rows, intra, q_dec, decay, update in local:
            out_ref[rows, :] = intra + lax.dot_general(
                q_dec, st.astype(MXU_DTYPE), nt, preferred_element_type=jnp.float32)
            st = decay * st + update
        st_ref[...] = st

    def body(n, _):
        fwd = [pl.multiple_of((n * group + i) * c, c) for i in range(group)]
        bwd = [pl.multiple_of((n_chunks - 1 - n * group - i) * c, c) for i in range(group)]
        direction(fwd, zf_ref, lbf_ref, sf_ref, of_ref, 0)
        direction(bwd, zb_ref, lbb_ref, sb_ref, ob_ref, 1)
        return 0

    lax.fori_loop(0, n_chunks // group, body, 0)

    def finish(i, _):
        rows = pl.ds(pl.multiple_of(i * out_rows, out_rows), out_rows)
        o = _rms(of_ref[rows, :] + ob_ref[rows, :], gn_ref[...])
        gate = gate_ref[rows, :].astype(jnp.float32)
        o_ref[rows, :] = (o * (gate / (1.0 + jnp.exp(-gate)))).astype(o_ref.dtype)
        return 0

    lax.fori_loop(0, t // out_rows, finish, 0)


def _hgrn(proj, zgates, lb, gn):
    b, t, _ = proj.shape
    h = N_HEADS
    c = _tile(t, HG_CHUNK)
    expo, mask, n_levels = _hgrn_constants(c)
    expo = jnp.asarray(expo, MXU_DTYPE)
    mask = jnp.asarray(mask, jnp.float32)
    out_rows = _tile(t, 512)
    group = HG_GROUP if (t // c) % HG_GROUP == 0 else 1
    head_spec = lambda off: pl.BlockSpec((None, t, HEAD_W), lambda bi, hi: (bi, 0, off + hi))
    vec_spec = pl.BlockSpec((None, 1, HEAD_W), lambda bi, hi: (hi, 0, 0))
    return pl.pallas_call(
        functools.partial(_hgrn_kernel, chunk=c, group=group, n_levels=n_levels,
                          out_rows=out_rows),
        grid=(b, h),
        in_specs=[head_spec(3 * h), head_spec(4 * h), head_spec(5 * h),
                  head_spec(0), head_spec(h), vec_spec, vec_spec, vec_spec,
                  pl.BlockSpec(expo.shape, lambda bi, hi: (0, 0, 0)),
                  pl.BlockSpec(mask.shape, lambda bi, hi: (0, 0, 0, 0))],
        out_specs=pl.BlockSpec((None, t, HEAD_W), lambda bi, hi: (bi, 0, hi)),
        out_shape=jax.ShapeDtypeStruct((b, t, h * HEAD_W), MXU_DTYPE),
        scratch_shapes=[pltpu.VMEM((t, HEAD_W), jnp.float32),
                        pltpu.VMEM((t, HEAD_W), jnp.float32),
                        pltpu.VMEM((HEAD_W, HEAD_W), jnp.float32),
                        pltpu.VMEM((HEAD_W, HEAD_W), jnp.float32)],
        compiler_params=_params("parallel", "arbitrary"),
    )(proj, proj, proj, zgates, zgates,
      lb[0].reshape(h, 1, HEAD_W), lb[1].reshape(h, 1, HEAD_W), gn.reshape(h, 1, HEAD_W),
      expo, mask)


def _conv_kernel(bg_ref, c_ref, h_ref, cp_ref, hp_ref, cn_ref, hn_ref, cw_ref, w_ref,
                 r_ref, o_ref, y_ref, u_ref, *, tm, cw, tiles_per_seq):
    i = pl.program_id(0)

    @pl.when(pl.program_id(1) == 0)
    def _():
        first = (i % tiles_per_seq) == 0
        last = (i % tiles_per_seq) == tiles_per_seq - 1
        d = c_ref.shape[1]

        def cols(n, _):
            cs = pl.ds(pl.multiple_of(n * cw, cw), cw)
            f32 = lambda ref, rows: ref[rows, cs].astype(jnp.float32)
            u = f32(c_ref, slice(None)) * f32(h_ref, slice(None))
            last_row = slice(CONV_HALO - 1, CONV_HALO)
            up = f32(cp_ref, last_row) * f32(hp_ref, last_row)
            un = f32(cn_ref, slice(0, 1)) * f32(hn_ref, slice(0, 1))
            u_ref[8:tm + 8, :] = u
            u_ref[7:8, :] = jnp.where(first, 0.0, up)
            u_ref[tm + 8:tm + 9, :] = jnp.where(last, 0.0, un)
            w = cw_ref[:, cs]
            conv = (w[0:1] * u_ref[7:tm + 7, :] + w[1:2] * u
                    + w[2:3] * u_ref[9:tm + 9, :])
            y_ref[:, cs] = (f32(bg_ref, slice(None)) * conv).astype(y_ref.dtype)
            return 0

        lax.fori_loop(0, d // cw, cols, 0)

    o_ref[...] = r_ref[...] + jnp.dot(y_ref[...], w_ref[...],
                                      preferred_element_type=jnp.float32)


def _conv_mixer(proj, conv_w, w_out, res, seq, tm=512, tn=2048, cw=256):
    m, d3 = proj.shape
    d = d3 // 3
    n = w_out.shape[1]
    tm, tn, cw = _tile(seq, tm), _tile(n, tn), _tile(d, cw)
    tiles_per_seq = seq // tm
    hb = tm // CONV_HALO
    n_hb = m // CONV_HALO
    main = lambda sec: pl.BlockSpec((tm, d), lambda i, j: (i, sec))
    prev = lambda sec: pl.BlockSpec((CONV_HALO, d),
                                    lambda i, j: (jnp.maximum(i * hb - 1, 0), sec))
    nxt = lambda sec: pl.BlockSpec((CONV_HALO, d),
                                   lambda i, j: (jnp.minimum((i + 1) * hb, n_hb - 1), sec))
    return pl.pallas_call(
        functools.partial(_conv_kernel, tm=tm, cw=cw, tiles_per_seq=tiles_per_seq),
        grid=(m // tm, n // tn),
        in_specs=[main(0), main(1), main(2), prev(1), prev(2), nxt(1), nxt(2),
                  pl.BlockSpec(conv_w.shape, lambda i, j: (0, 0)),
                  pl.BlockSpec((d, tn), lambda i, j: (0, j)),
                  pl.BlockSpec((tm, tn), lambda i, j: (i, j))],
        out_specs=pl.BlockSpec((tm, tn), lambda i, j: (i, j)),
        out_shape=jax.ShapeDtypeStruct((m, n), jnp.float32),
        scratch_shapes=[pltpu.VMEM((tm, d), MXU_DTYPE),
                        pltpu.VMEM((tm + 16, cw), jnp.float32)],
        compiler_params=_params("parallel", "arbitrary"),
    )(proj, proj, proj, proj, proj, proj, proj, conv_w, w_out, res)


def kernel(x, mix_norm_g, mlp_norm_g, final_norm_g, ab_w_in, ab_w_out, diff_lambda,
           diff_subln_g, hgrn_lb, hgrn_norm_g, conv_w_in, conv_w, conv_w_out,
           mlp_w_up, mlp_w_down):
    bsz, seq, d = x.shape
    m = bsz * seq
    h = N_HEADS
    w_attn = h * HEAD_W
    n_bf = 6 * w_attn
    cast = lambda w: w.astype(MXU_DTYPE)

    hres = x.reshape(m, d)

    proj, zgates = _norm_matmul_split(hres, mix_norm_g[0], cast(ab_w_in[0]), n_bf,
                                      MXU_DTYPE, jnp.float32)
    proj = proj.reshape(bsz, seq, n_bf)
    zgates = zgates.reshape(bsz, seq, 2 * w_attn)

    lam_init = 0.8 - 0.6 * math.exp(-0.3 * 0)
    slopes = 2.0 ** (-8.0 * jnp.arange(1, h + 1, dtype=jnp.float32) / h)
    slopes = jnp.broadcast_to(slopes[:, None, None], (h, 1, HEAD_W))
    oa = _attention(proj, slopes, diff_lambda[0].astype(jnp.float32), diff_subln_g[0], lam_init)

    lb = jnp.cumsum(jax.nn.softmax(hgrn_lb.astype(jnp.float32), axis=1), axis=1)[:, 0]
    ob = _hgrn(proj, zgates, lb, hgrn_norm_g[0])

    hres = _proj2_residual(oa.reshape(m, w_attn), ob.reshape(m, w_attn),
                           cast(ab_w_out[0]), hres)
    hres = _mlp(hres, mlp_norm_g[0], cast(mlp_w_up[0]), cast(mlp_w_down[0]),
                final_norm_g, False)

    proj = _norm_matmul(hres, mix_norm_g[1], cast(conv_w_in[0]), MXU_DTYPE)
    hres = _conv_mixer(proj, conv_w[0], cast(conv_w_out[0]), hres, seq)
    hres = _mlp(hres, mlp_norm_g[1], cast(mlp_w_up[1]), cast(mlp_w_down[1]),
                final_norm_g, True)
    return hres.reshape(bsz, seq, d)
```

```python
import functools
import math

import numpy as np
import jax
import jax.numpy as jnp
from jax import lax
from jax.experimental import pallas as pl
from jax.experimental.pallas import tpu as pltpu

EPS = 1e-6
N_HEADS = 8
HEAD_W = 128
DA = 64
HG_CHUNK = 64
HG_GROUP = 8
CONV_HALO = 16
MXU_DTYPE = jnp.bfloat16
LOG2E = 1.4426950408889634
ATTN_TQ = 256
ATTN_TK = 512
ATTN_UNROLL = 4
ATTN_UNDERFLOW = 160.0
VMEM_LIMIT = 56 * 1024 * 1024


def _tile(dim, pref):
    return pref if dim % pref == 0 else dim


def _rms(x, g):
    ms = jnp.mean(x * x, axis=-1, keepdims=True)
    return x * lax.rsqrt(ms + EPS) * g


def _params(*sem):
    return pltpu.CompilerParams(dimension_semantics=sem, vmem_limit_bytes=VMEM_LIMIT)


def _norm_matmul_split_kernel(x_ref, g_ref, w_ref, lo_ref, hi_ref, hn_ref, *, n_lo):
    j = pl.program_id(1)

    @pl.when(j == 0)
    def _():
        hn_ref[...] = _rms(x_ref[...], g_ref[...]).astype(hn_ref.dtype)

    def product():
        return jnp.dot(hn_ref[...], w_ref[...], preferred_element_type=jnp.float32)

    @pl.when(j < n_lo)
    def _():
        lo_ref[...] = product().astype(lo_ref.dtype)

    @pl.when(j >= n_lo)
    def _():
        hi_ref[...] = product().astype(hi_ref.dtype)


def _norm_matmul_split(x, g, w, n_split, lo_dtype, hi_dtype, tm=1024, tn=1024):
    m, k = x.shape
    n = w.shape[1]
    tm, tn = _tile(m, tm), _tile(n_split, tn)
    assert n_split % tn == 0 and (n - n_split) % tn == 0
    n_lo = n_split // tn
    return pl.pallas_call(
        functools.partial(_norm_matmul_split_kernel, n_lo=n_lo),
        grid=(m // tm, n // tn),
        in_specs=[pl.BlockSpec((tm, k), lambda i, j: (i, 0)),
                  pl.BlockSpec((1, k), lambda i, j: (0, 0)),
                  pl.BlockSpec((k, tn), lambda i, j: (0, j))],
        out_specs=[pl.BlockSpec((tm, tn), lambda i, j: (i, jnp.minimum(j, n_lo - 1))),
                   pl.BlockSpec((tm, tn), lambda i, j: (i, jnp.maximum(j - n_lo, 0)))],
        out_shape=[jax.ShapeDtypeStruct((m, n_split), lo_dtype),
                   jax.ShapeDtypeStruct((m, n - n_split), hi_dtype)],
        scratch_shapes=[pltpu.VMEM((tm, k), MXU_DTYPE)],
        compiler_params=_params("parallel", "arbitrary"),
    )(x, g.reshape(1, k), w)


def _proj2_kernel(a_ref, b_ref, wa_ref, wb_ref, r_ref, o_ref):
    acc = jnp.dot(a_ref[...], wa_ref[...], preferred_element_type=jnp.float32)
    acc += jnp.dot(b_ref[...], wb_ref[...], preferred_element_type=jnp.float32)
    o_ref[...] = r_ref[...] + acc


def _proj2_residual(a, b, w, res, tm=512, tn=2048):
    m, ka = a.shape
    kb = b.shape[1]
    n = w.shape[1]
    assert ka == kb and w.shape[0] == ka + kb
    tm, tn = _tile(m, tm), _tile(n, tn)
    return pl.pallas_call(
        _proj2_kernel,
        grid=(m // tm, n // tn),
        in_specs=[pl.BlockSpec((tm, ka), lambda i, j: (i, 0)),
                  pl.BlockSpec((tm, kb), lambda i, j: (i, 0)),
                  pl.BlockSpec((ka, tn), lambda i, j: (0, j)),
                  pl.BlockSpec((kb, tn), lambda i, j: (1, j)),
                  pl.BlockSpec((tm, tn), lambda i, j: (i, j))],
        out_specs=pl.BlockSpec((tm, tn), lambda i, j: (i, j)),
        out_shape=jax.ShapeDtypeStruct((m, n), jnp.float32),
        compiler_params=_params("parallel", "arbitrary"),
    )(a, b, w, w, res)


def _mlp_kernel(h_ref, g_ref, wu_ref, wd_ref, gf_ref, o_ref, hn_ref, *, final_norm):
    j = pl.program_id(1)

    @pl.when(j == 0)
    def _():
        h = h_ref[...]
        hn_ref[...] = _rms(h, g_ref[...]).astype(hn_ref.dtype)
        o_ref[...] = h

    a = jnp.dot(hn_ref[...], wu_ref[...], preferred_element_type=jnp.float32)
    a = jnp.maximum(a, 0.0)
    a = (a * a).astype(wd_ref.dtype)
    o_ref[...] += jnp.dot(a, wd_ref[...], preferred_element_type=jnp.float32)

    if final_norm:
        @pl.when(j == pl.num_programs(1) - 1)
        def _():
            o_ref[...] = _rms(o_ref[...], gf_ref[...])


def _mlp(h, g, w_up, w_down, g_final, final_norm, tm=1024, tf=512):
    m, d = h.shape
    f = w_up.shape[1]
    tm, tf = _tile(m, tm), _tile(f, tf)
    return pl.pallas_call(
        functools.partial(_mlp_kernel, final_norm=final_norm),
        grid=(m // tm, f // tf),
        in_specs=[pl.BlockSpec((tm, d), lambda i, j: (i, 0)),
                  pl.BlockSpec((1, d), lambda i, j: (0, 0)),
                  pl.BlockSpec((d, tf), lambda i, j: (0, j)),
                  pl.BlockSpec((tf, d), lambda i, j: (j, 0)),
                  pl.BlockSpec((1, d), lambda i, j: (0, 0))],
        out_specs=pl.BlockSpec((tm, d), lambda i, j: (i, 0)),
        out_shape=jax.ShapeDtypeStruct((m, d), jnp.float32),
        scratch_shapes=[pltpu.VMEM((tm, d), MXU_DTYPE)],
        compiler_params=_params("parallel", "arbitrary"),
    )(h, g.reshape(1, d), w_up, w_down, g_final.reshape(1, d))


def _attn_kernel(q_ref, qall_ref, k_ref, v_ref, slope_ref, lam_ref, g_ref, o_ref,
                 s0_ref, s1_ref, m_ref, mfin0_ref, mfin1_ref, acc_ref, bias_ref, vaug_ref,
                 qq_ref, win_ref, *, tq, tk, unroll, lam_init):
    t = k_ref.shape[0]
    nk, nq = t // tk, t // tq
    rows = 2 * tq
    bi, qi = pl.program_id(1), pl.program_id(2)
    q0 = qi * tq
    nt = (((1,), (1,)), ((), ()))
    c = slope_ref[...][:, :1] * LOG2E
    s_refs, mfin_refs = (s0_ref, s1_ref), (mfin0_ref, mfin1_ref)

    strip_rows, strip_w = bias_ref.shape
    strip_origin = t + tq - strip_rows

    @pl.when((bi == 0) & (qi == 0))
    def _():
        r = lax.broadcasted_iota(jnp.int32, (strip_rows, HEAD_W), 0)
        u = lax.broadcasted_iota(jnp.int32, (strip_rows, HEAD_W), 1)
        rel = (r - u + strip_origin).astype(jnp.float32)

        def fill(a, _):
            u0 = pl.multiple_of(a * HEAD_W, HEAD_W)
            bias_ref[:, pl.ds(u0, HEAD_W)] = -c * jnp.abs(rel - u0.astype(jnp.float32))
            return 0

        lax.fori_loop(0, strip_w // HEAD_W, fill, 0)

    @pl.when(qi == 0)
    def _():
        vaug_ref[:, :HEAD_W] = v_ref[...]
        vaug_ref[:, HEAD_W:] = jnp.ones((t, HEAD_W), vaug_ref.dtype)

        def max_sq_norm(ref):
            x = ref[...].astype(jnp.float32)
            return jnp.max(jnp.sum(x * x, axis=1, keepdims=True), axis=0, keepdims=True)

        bound = jnp.sqrt(max_sq_norm(qall_ref) * max_sq_norm(k_ref)) * (
            DA ** -0.5 * LOG2E * 1.01)
        dist = jnp.minimum((2.0 * bound + ATTN_UNDERFLOW) / c, 2.0 * t)
        n_win = jnp.floor((2.0 * dist + (tq - 1)) / tk) + 2.0
        n_win = jnp.minimum(jnp.ceil(n_win / unroll) * unroll, 1.0 * nk)
        win_ref[0] = n_win[0, 0].astype(jnp.int32)
        win_ref[1] = jnp.floor(dist)[0, 0].astype(jnp.int32)

    n_win, dist = win_ref[0], win_ref[1]

    def first_tile(start):
        return jnp.clip(jnp.maximum(start - dist, 0) // tk, 0, nk - n_win)

    lo_cur = first_tile(q0)
    lo_prev = first_tile(q0 - tq)

    def sweep1(cur, r):
        k0 = pl.multiple_of((lo_cur + r) * tk, tk)
        s = lax.dot_general(qq_ref[...], k_ref[pl.ds(k0, tk), :], nt,
                            preferred_element_type=jnp.float32)
        bias = [bias_ref[:, pl.ds(pl.multiple_of(k0 - q0 - rb * strip_rows + strip_origin,
                                                 HEAD_W), tk)]
                for rb in range(tq // strip_rows)]
        st = s + jnp.concatenate(bias + bias, axis=0)
        s_refs[cur][:, pl.ds(pl.multiple_of(r * tk, tk), tk)] = st
        part = functools.reduce(
            jnp.maximum, [st[:, a * HEAD_W:(a + 1) * HEAD_W] for a in range(tk // HEAD_W)])
        m_ref[...] = jnp.maximum(m_ref[...], part)

    def sweep2(prev, r):
        k0 = pl.multiple_of((lo_prev + r) * tk, tk)
        cols = pl.ds(pl.multiple_of(r * tk, tk), tk)
        for mp in range(2):
            rs = slice(mp * tq, (mp + 1) * tq)
            m_rep = mfin_refs[prev][rs, :]
            p = jnp.exp2(s_refs[prev][rs, cols] - jnp.concatenate([m_rep] * (tk // HEAD_W), axis=1))
            acc_ref[mp] += jnp.dot(p.astype(vaug_ref.dtype), vaug_ref[pl.ds(k0, tk), :],
                                   preferred_element_type=jnp.float32)

    def start_sweep1():
        q = q_ref[...].astype(jnp.float32) * (DA ** -0.5 * LOG2E)
        lane = lax.broadcasted_iota(jnp.int32, q.shape, 1)
        qq_ref[...] = jnp.concatenate([jnp.where(lane < DA, q, 0.0),
                                       jnp.where(lane >= DA, q, 0.0)],
                                      axis=0).astype(qq_ref.dtype)
        m_ref[...] = jnp.full(m_ref.shape, -jnp.inf, jnp.float32)

    def end_sweep1(cur):
        m = jnp.max(m_ref[...], axis=1, keepdims=True)
        mfin_refs[cur][...] = jnp.broadcast_to(m, (rows, HEAD_W))

    def start_sweep2():
        acc_ref[...] = jnp.zeros(acc_ref.shape, jnp.float32)

    def end_sweep2():
        lp = lam_ref[...]
        lam = (jnp.exp(jnp.sum(lp[0:1] * lp[1:2], axis=1, keepdims=True))
               - jnp.exp(jnp.sum(lp[2:3] * lp[3:4], axis=1, keepdims=True)) + lam_init)
        o = [acc_ref[mp][:, :HEAD_W] / acc_ref[mp][:, HEAD_W:] for mp in range(2)]
        o = o[0] - lam * o[1]
        o_ref[...] = (_rms(o, g_ref[...]) * (1.0 - lam_init)).astype(o_ref.dtype)

    def loop(*parts):
        def body(i, _):
            for part in parts:
                for u in range(unroll):
                    part(i * unroll + u)
            return 0
        lax.fori_loop(0, n_win // unroll, body, 0)

    @pl.when(qi == 0)
    def _():
        start_sweep1()
        loop(functools.partial(sweep1, 0))
        end_sweep1(0)

    for cur in range(2):
        @pl.when((qi > 0) & (qi < nq) & (qi % 2 == cur))
        def _():
            start_sweep1()
            start_sweep2()
            loop(functools.partial(sweep1, cur), functools.partial(sweep2, 1 - cur))
            end_sweep1(cur)
            end_sweep2()

    @pl.when(qi == nq)
    def _():
        start_sweep2()
        loop(functools.partial(sweep2, (nq - 1) % 2))
        end_sweep2()


def _attention(proj, slopes, lam_p, subln_g, lam_init):
    b, t, _ = proj.shape
    tq, tk = _tile(t, ATTN_TQ), _tile(t, ATTN_TK)
    nq = t // tq
    strip_rows = min(tq, HEAD_W)
    h = N_HEADS
    return pl.pallas_call(
        functools.partial(_attn_kernel, tq=tq, tk=tk, lam_init=lam_init,
                          unroll=ATTN_UNROLL if (t // tk) % ATTN_UNROLL == 0 else 1),
        grid=(h, b, nq + 1),
        in_specs=[pl.BlockSpec((None, tq, HEAD_W),
                               lambda hi, bi, qi: (bi, jnp.minimum(qi, nq - 1), hi)),
                  pl.BlockSpec((None, t, HEAD_W), lambda hi, bi, qi: (bi, 0, hi)),
                  pl.BlockSpec((None, t, HEAD_W), lambda hi, bi, qi: (bi, 0, h + hi)),
                  pl.BlockSpec((None, t, HEAD_W), lambda hi, bi, qi: (bi, 0, 2 * h + hi)),
                  pl.BlockSpec((None, 1, HEAD_W), lambda hi, bi, qi: (hi, 0, 0)),
                  pl.BlockSpec(lam_p.shape, lambda hi, bi, qi: (0, 0)),
                  pl.BlockSpec((1, HEAD_W), lambda hi, bi, qi: (0, 0))],
        out_specs=pl.BlockSpec((None, tq, HEAD_W),
                               lambda hi, bi, qi: (bi, jnp.maximum(qi - 1, 0), hi)),
        out_shape=jax.ShapeDtypeStruct((b, t, h * HEAD_W), MXU_DTYPE),
        scratch_shapes=[pltpu.VMEM((2 * tq, t), jnp.float32),
                        pltpu.VMEM((2 * tq, t), jnp.float32),
                        pltpu.VMEM((2 * tq, HEAD_W), jnp.float32),
                        pltpu.VMEM((2 * tq, HEAD_W), jnp.float32),
                        pltpu.VMEM((2 * tq, HEAD_W), jnp.float32),
                        pltpu.VMEM((2, tq, 2 * HEAD_W), jnp.float32),
                        pltpu.VMEM((strip_rows, 2 * t + tq - strip_rows), jnp.float32),
                        pltpu.VMEM((t, 2 * HEAD_W), MXU_DTYPE),
                        pltpu.VMEM((2 * tq, HEAD_W), MXU_DTYPE),
                        pltpu.SMEM((2,), jnp.int32)],
        compiler_params=_params("parallel", "arbitrary", "arbitrary"),
    )(proj, proj, proj, proj, slopes, lam_p, subln_g.reshape(1, HEAD_W))


def _hgrn_constants(c):
    levels = []
    size = 2
    while size <= c:
        levels.append(size)
        size *= 2
    expo = np.zeros((len(levels) + 1, c, c), np.float32)
    mask = np.zeros((len(levels) + 1, c, c), np.float32)
    for li, size in enumerate(levels):
        half = size // 2
        for t in range(c):
            pos = t % size
            bd = t - pos + half - 1
            if pos >= half:
                expo[li, t, bd + 1:t + 1] = 1.0
                mask[li, t, t - pos:t - pos + half] = 1.0
            else:
                expo[li, t, t + 1:bd + 1] = 1.0
    expo[-1] = np.tril(np.ones((c, c), np.float32))
    mask[-1] = np.eye(c, dtype=np.float32)

    def pack(e):
        e = e.reshape(-1, c)
        return np.concatenate([e, e, e], axis=1)

    return (np.stack([pack(expo), pack(expo[:, ::-1, ::-1])]),
            np.stack([mask, mask[:, ::-1, ::-1]]), len(levels))


def _split3(x):
    hi = x.astype(jnp.bfloat16)
    r1 = x - hi.astype(jnp.float32)
    mid = r1.astype(jnp.bfloat16)
    lo = (r1 - mid.astype(jnp.float32)).astype(jnp.bfloat16)
    return jnp.concatenate([hi, mid, lo], axis=0)


def _hgrn_kernel(q_ref, v_ref, gate_ref, zf_ref, zb_ref, lbf_ref, lbb_ref, gn_ref,
                 expo_ref, mask_ref, o_ref, of_ref, ob_ref, sf_ref, sb_ref,
                 *, chunk, group, n_levels, out_rows):
    t = q_ref.shape[0]
    c = chunk
    n_chunks = t // c
    nt = (((1,), (1,)), ((), ()))
    tn = (((0,), (0,)), ((), ()))
    sf_ref[...] = jnp.zeros_like(sf_ref)
    sb_ref[...] = jnp.zeros_like(sb_ref)

    def chunk_local(r0, z_ref, lb_ref, d):
        rows = pl.ds(r0, c)
        z = z_ref[rows, :]
        lb = lb_ref[...]
        e = jnp.exp(-jnp.abs(z))
        r = 1.0 / (1.0 + e)
        er = e * r
        pos = z >= 0.0
        f = lb + (1.0 - lb) * jnp.where(pos, r, er)
        kk = (1.0 - lb) * jnp.where(pos, er, r)
        g = jnp.log(f)
        ex_all = jnp.dot(expo_ref[d], _split3(g), preferred_element_type=jnp.float32)
        b = ex_all[n_levels * c:]
        ex_all = jnp.exp(ex_all)
        qf = q_ref[rows, :].astype(jnp.float32)
        v = v_ref[rows, :]
        att = mask_ref[d, n_levels] * lax.dot_general(
            qf.astype(MXU_DTYPE), kk.astype(MXU_DTYPE), nt,
            preferred_element_type=jnp.float32)
        for li in range(n_levels):
            ex = ex_all[li * c:(li + 1) * c]
            att += mask_ref[d, li] * lax.dot_general(
                (qf * ex).astype(MXU_DTYPE), (kk * ex).astype(MXU_DTYPE), nt,
                preferred_element_type=jnp.float32)
        intra = jnp.dot(att.astype(MXU_DTYPE), v, preferred_element_type=jnp.float32)
        q_dec = (qf * ex_all[n_levels * c:]).astype(MXU_DTYPE)
        b_last = b[c - 1:c] if d == 0 else b[0:1]
        kd = (kk * jnp.exp(b_last - b)).astype(MXU_DTYPE)
        update = lax.dot_general(v, kd, tn, preferred_element_type=jnp.float32)
        return rows, intra, q_dec, jnp.exp(b_last), update

    def direction(starts, z_ref, lb_ref, st_ref, out_ref, d):
        local = [chunk_local(r0, z_ref, lb_ref, d) for r0 in starts]
        st = st_ref[...]
        for rows, intra, q_dec, decay, update in local:
            out_ref[rows, :] = intra + lax.dot_general(
                q_dec, st.astype(MXU_DTYPE), nt, preferred_element_type=jnp.float32)
            st = decay * st + update
        st_ref[...] = st

    def body(n, _):
        fwd = [pl.multiple_of((n * group + i) * c, c) for i in range(group)]
        bwd = [pl.multiple_of((n_chunks - 1 - n * group - i) * c, c) for i in range(group)]
        direction(fwd, zf_ref, lbf_ref, sf_ref, of_ref, 0)
        direction(bwd, zb_ref, lbb_ref, sb_ref, ob_ref, 1)
        return 0

    lax.fori_loop(0, n_chunks // group, body, 0)

    def finish(i, _):
        rows = pl.ds(pl.multiple_of(i * out_rows, out_rows), out_rows)
        o = _rms(of_ref[rows, :] + ob_ref[rows, :], gn_ref[...])
        gate = gate_ref[rows, :].astype(jnp.float32)
        o_ref[rows, :] = (o * (gate / (1.0 + jnp.exp(-gate)))).astype(o_ref.dtype)
        return 0

    lax.fori_loop(0, t // out_rows, finish, 0)


def _hgrn(proj, zgates, lb, gn):
    b, t, _ = proj.shape
    h = N_HEADS
    c = _tile(t, HG_CHUNK)
    expo, mask, n_levels = _hgrn_constants(c)
    expo = jnp.asarray(expo, MXU_DTYPE)
    mask = jnp.asarray(mask, jnp.float32)
    out_rows = _tile(t, 512)
    group = HG_GROUP if (t // c) % HG_GROUP == 0 else 1
    head_spec = lambda off: pl.BlockSpec((None, t, HEAD_W), lambda bi, hi: (bi, 0, off + hi))
    vec_spec = pl.BlockSpec((None, 1, HEAD_W), lambda bi, hi: (hi, 0, 0))
    return pl.pallas_call(
        functools.partial(_hgrn_kernel, chunk=c, group=group, n_levels=n_levels,
                          out_rows=out_rows),
        grid=(b, h),
        in_specs=[head_spec(3 * h), head_spec(4 * h), head_spec(5 * h),
                  head_spec(0), head_spec(h), vec_spec, vec_spec, vec_spec,
                  pl.BlockSpec(expo.shape, lambda bi, hi: (0, 0, 0)),
                  pl.BlockSpec(mask.shape, lambda bi, hi: (0, 0, 0, 0))],
        out_specs=pl.BlockSpec((None, t, HEAD_W), lambda bi, hi: (bi, 0, hi)),
        out_shape=jax.ShapeDtypeStruct((b, t, h * HEAD_W), MXU_DTYPE),
        scratch_shapes=[pltpu.VMEM((t, HEAD_W), jnp.float32),
                        pltpu.VMEM((t, HEAD_W), jnp.float32),
                        pltpu.VMEM((HEAD_W, HEAD_W), jnp.float32),
                        pltpu.VMEM((HEAD_W, HEAD_W), jnp.float32)],
        compiler_params=_params("parallel", "arbitrary"),
    )(proj, proj, proj, zgates, zgates,
      lb[0].reshape(h, 1, HEAD_W), lb[1].reshape(h, 1, HEAD_W), gn.reshape(h, 1, HEAD_W),
      expo, mask)


def _conv_inproj_kernel(x_ref, g_ref, w_ref, bg_ref, u_ref, hn_ref, *, cc):
    @pl.when(pl.program_id(1) == 0)
    def _():
        hn_ref[...] = _rms(x_ref[...], g_ref[...]).astype(hn_ref.dtype)

    r = jnp.dot(hn_ref[...], w_ref[...], preferred_element_type=jnp.float32)
    bg_ref[...] = r[:, :cc].astype(bg_ref.dtype)
    u_ref[...] = (r[:, cc:2 * cc] * r[:, 2 * cc:]).astype(u_ref.dtype)


def _conv_inproj(x, g, w, tm=1024, cc=512):
    m, k = x.shape
    d = w.shape[1] // 3
    tm, cc = _tile(m, tm), _tile(d, cc)
    nb = d // cc
    w = w.reshape(k, 3, nb, cc).transpose(0, 2, 1, 3).reshape(k, 3 * d).astype(MXU_DTYPE)
    return pl.pallas_call(
        functools.partial(_conv_inproj_kernel, cc=cc),
        grid=(m // tm, nb),
        in_specs=[pl.BlockSpec((tm, k), lambda i, j: (i, 0)),
                  pl.BlockSpec((1, k), lambda i, j: (0, 0)),
                  pl.BlockSpec((k, 3 * cc), lambda i, j: (0, j))],
        out_specs=[pl.BlockSpec((tm, cc), lambda i, j: (i, j)),
                   pl.BlockSpec((tm, cc), lambda i, j: (i, j))],
        out_shape=[jax.ShapeDtypeStruct((m, d), MXU_DTYPE),
                   jax.ShapeDtypeStruct((m, d), MXU_DTYPE)],
        scratch_shapes=[pltpu.VMEM((tm, k), MXU_DTYPE)],
        compiler_params=_params("parallel", "arbitrary"),
    )(x, g.reshape(1, k), w)


def _conv_kernel(bg_ref, u_ref, up_ref, un_ref, cw_ref, w_ref, r_ref, o_ref, us_ref,
                 *, tm, cw, tiles_per_seq):
    i = pl.program_id(0)
    first = (i % tiles_per_seq) == 0
    last = (i % tiles_per_seq) == tiles_per_seq - 1
    d = u_ref.shape[1]
    ys = []
    for n in range(d // cw):
        cs = slice(n * cw, (n + 1) * cw)
        u = u_ref[:, cs].astype(jnp.float32)
        us_ref[n, 8:tm + 8, :] = u
        us_ref[n, 7:8, :] = jnp.where(
            first, 0.0, up_ref[CONV_HALO - 1:CONV_HALO, cs].astype(jnp.float32))
        us_ref[n, tm + 8:tm + 9, :] = jnp.where(
            last, 0.0, un_ref[0:1, cs].astype(jnp.float32))
        w = cw_ref[:, cs]
        conv = (w[0:1] * us_ref[n, 7:tm + 7, :] + w[1:2] * u
                + w[2:3] * us_ref[n, 9:tm + 9, :])
        ys.append((bg_ref[:, cs].astype(jnp.float32) * conv).astype(w_ref.dtype))
    o_ref[...] = r_ref[...] + jnp.dot(jnp.concatenate(ys, axis=1), w_ref[...],
                                      preferred_element_type=jnp.float32)


def _conv_mixer(bg, u, conv_w, w_out, res, seq, tm=512, cw=256):
    m, d = u.shape
    n = w_out.shape[1]
    tm, tn, cw = _tile(seq, tm), n, _tile(d, cw)
    tiles_per_seq = seq // tm
    hb = tm // CONV_HALO
    n_hb = m // CONV_HALO
    main = pl.BlockSpec((tm, d), lambda i, j: (i, 0))
    prev = pl.BlockSpec((CONV_HALO, d), lambda i, j: (jnp.maximum(i * hb - 1, 0), 0))
    nxt = pl.BlockSpec((CONV_HALO, d), lambda i, j: (jnp.minimum((i + 1) * hb, n_hb - 1), 0))
    return pl.pallas_call(
        functools.partial(_conv_kernel, tm=tm, cw=cw, tiles_per_seq=tiles_per_seq),
        grid=(m // tm, n // tn),
        in_specs=[main, main, prev, nxt,
                  pl.BlockSpec(conv_w.shape, lambda i, j: (0, 0)),
                  pl.BlockSpec((d, tn), lambda i, j: (0, j)),
                  pl.BlockSpec((tm, tn), lambda i, j: (i, j))],
        out_specs=pl.BlockSpec((tm, tn), lambda i, j: (i, j)),
        out_shape=jax.ShapeDtypeStruct((m, n), jnp.float32),
        scratch_shapes=[pltpu.VMEM((d // cw, tm + 16, cw), jnp.float32)],
        compiler_params=_params("parallel", "arbitrary"),
    )(bg, u, u, u, conv_w, w_out, res)


def kernel(x, mix_norm_g, mlp_norm_g, final_norm_g, ab_w_in, ab_w_out, diff_lambda,
           diff_subln_g, hgrn_lb, hgrn_norm_g, conv_w_in, conv_w, conv_w_out,
           mlp_w_up, mlp_w_down):
    bsz, seq, d = x.shape
    m = bsz * seq
    h = N_HEADS
    w_attn = h * HEAD_W
    n_bf = 6 * w_attn
    cast = lambda w: w.astype(MXU_DTYPE)

    hres = x.reshape(m, d)

    proj, zgates = _norm_matmul_split(hres, mix_norm_g[0], cast(ab_w_in[0]), n_bf,
                                      MXU_DTYPE, jnp.float32)
    proj = proj.reshape(bsz, seq, n_bf)
    zgates = zgates.reshape(bsz, seq, 2 * w_attn)

    lam_init = 0.8 - 0.6 * math.exp(-0.3 * 0)
    slopes = 2.0 ** (-8.0 * jnp.arange(1, h + 1, dtype=jnp.float32) / h)
    slopes = jnp.broadcast_to(slopes[:, None, None], (h, 1, HEAD_W))
    oa = _attention(proj, slopes, diff_lambda[0].astype(jnp.float32), diff_subln_g[0], lam_init)

    lb = jnp.cumsum(jax.nn.softmax(hgrn_lb.astype(jnp.float32), axis=1), axis=1)[:, 0]
    ob = _hgrn(proj, zgates, lb, hgrn_norm_g[0])

    hres = _proj2_residual(oa.reshape(m, w_attn), ob.reshape(m, w_attn),
                           cast(ab_w_out[0]), hres)
    hres = _mlp(hres, mlp_norm_g[0], cast(mlp_w_up[0]), cast(mlp_w_down[0]),
                final_norm_g, False)

    bgate, u = _conv_inproj(hres, mix_norm_g[1], conv_w_in[0])
    hres = _conv_mixer(bgate, u, conv_w[0], cast(conv_w_out[0]), hres, seq)
    hres = _mlp(hres, mlp_norm_g[1], cast(mlp_w_up[1]), cast(mlp_w_down[1]),
                final_norm_g, True)
    return hres.reshape(bsz, seq, d)
```

```python
import functools
import math

import numpy as np
import jax
import jax.numpy as jnp
from jax import lax
from jax.experimental import pallas as pl
from jax.experimental.pallas import tpu as pltpu

EPS = 1e-6
N_HEADS = 8
HEAD_W = 128
DA = 64
HG_CHUNK = 64
HG_GROUP = 8
CONV_HALO = 16
MXU_DTYPE = jnp.bfloat16
LOG2E = 1.4426950408889634
ATTN_TQ = 256
ATTN_TK = 512
ATTN_UNROLL = 4
ATTN_UNDERFLOW = 160.0
VMEM_LIMIT = 56 * 1024 * 1024


def _tile(dim, pref):
    return pref if dim % pref == 0 else dim


def _rms(x, g):
    ms = jnp.mean(x * x, axis=-1, keepdims=True)
    return x * lax.rsqrt(ms + EPS) * g


def _params(*sem):
    return pltpu.CompilerParams(dimension_semantics=sem, vmem_limit_bytes=VMEM_LIMIT)


def _norm_matmul_split_kernel(x_ref, g_ref, w_ref, lo_ref, hi_ref, hn_ref, *, n_lo):
    j = pl.program_id(1)

    @pl.when(j == 0)
    def _():
        hn_ref[...] = _rms(x_ref[...], g_ref[...]).astype(hn_ref.dtype)

    def product():
        return jnp.dot(hn_ref[...], w_ref[...], preferred_element_type=jnp.float32)

    @pl.when(j < n_lo)
    def _():
        lo_ref[...] = product().astype(lo_ref.dtype)

    @pl.when(j >= n_lo)
    def _():
        hi_ref[...] = product().astype(hi_ref.dtype)


def _norm_matmul_split(x, g, w, n_split, lo_dtype, hi_dtype, tm=1024, tn=1024):
    m, k = x.shape
    n = w.shape[1]
    tm, tn = _tile(m, tm), _tile(n_split, tn)
    assert n_split % tn == 0 and (n - n_split) % tn == 0
    n_lo = n_split // tn
    return pl.pallas_call(
        functools.partial(_norm_matmul_split_kernel, n_lo=n_lo),
        grid=(m // tm, n // tn),
        in_specs=[pl.BlockSpec((tm, k), lambda i, j: (i, 0)),
                  pl.BlockSpec((1, k), lambda i, j: (0, 0)),
                  pl.BlockSpec((k, tn), lambda i, j: (0, j))],
        out_specs=[pl.BlockSpec((tm, tn), lambda i, j: (i, jnp.minimum(j, n_lo - 1))),
                   pl.BlockSpec((tm, tn), lambda i, j: (i, jnp.maximum(j - n_lo, 0)))],
        out_shape=[jax.ShapeDtypeStruct((m, n_split), lo_dtype),
                   jax.ShapeDtypeStruct((m, n - n_split), hi_dtype)],
        scratch_shapes=[pltpu.VMEM((tm, k), MXU_DTYPE)],
        compiler_params=_params("parallel", "arbitrary"),
    )(x, g.reshape(1, k), w)


def _proj2_kernel(a_ref, b_ref, wa_ref, wb_ref, r_ref, o_ref):
    acc = jnp.dot(a_ref[...], wa_ref[...], preferred_element_type=jnp.float32)
    acc += jnp.dot(b_ref[...], wb_ref[...], preferred_element_type=jnp.float32)
    o_ref[...] = r_ref[...] + acc


def _proj2_residual(a, b, w, res, tm=512, tn=2048):
    m, ka = a.shape
    kb = b.shape[1]
    n = w.shape[1]
    assert ka == kb and w.shape[0] == ka + kb
    tm, tn = _tile(m, tm), _tile(n, tn)
    return pl.pallas_call(
        _proj2_kernel,
        grid=(m // tm, n // tn),
        in_specs=[pl.BlockSpec((tm, ka), lambda i, j: (i, 0)),
                  pl.BlockSpec((tm, kb), lambda i, j: (i, 0)),
                  pl.BlockSpec((ka, tn), lambda i, j: (0, j)),
                  pl.BlockSpec((kb, tn), lambda i, j: (1, j)),
                  pl.BlockSpec((tm, tn), lambda i, j: (i, j))],
        out_specs=pl.BlockSpec((tm, tn), lambda i, j: (i, j)),
        out_shape=jax.ShapeDtypeStruct((m, n), jnp.float32),
        compiler_params=_params("parallel", "arbitrary"),
    )(a, b, w, w, res)


def _mlp_kernel(h_ref, g_ref, wu_ref, wd_ref, gf_ref, o_ref, hn_ref, *, final_norm):
    j = pl.program_id(1)

    @pl.when(j == 0)
    def _():
        h = h_ref[...]
        hn_ref[...] = _rms(h, g_ref[...]).astype(hn_ref.dtype)
        o_ref[...] = h

    a = jnp.dot(hn_ref[...], wu_ref[...], preferred_element_type=jnp.float32)
    a = jnp.maximum(a, 0.0)
    a = (a * a).astype(wd_ref.dtype)
    o_ref[...] += jnp.dot(a, wd_ref[...], preferred_element_type=jnp.float32)

    if final_norm:
        @pl.when(j == pl.num_programs(1) - 1)
        def _():
            o_ref[...] = _rms(o_ref[...], gf_ref[...])


def _mlp(h, g, w_up, w_down, g_final, final_norm, tm=1024, tf=512):
    m, d = h.shape
    f = w_up.shape[1]
    tm, tf = _tile(m, tm), _tile(f, tf)
    return pl.pallas_call(
        functools.partial(_mlp_kernel, final_norm=final_norm),
        grid=(m // tm, f // tf),
        in_specs=[pl.BlockSpec((tm, d), lambda i, j: (i, 0)),
                  pl.BlockSpec((1, d), lambda i, j: (0, 0)),
                  pl.BlockSpec((d, tf), lambda i, j: (0, j)),
                  pl.BlockSpec((tf, d), lambda i, j: (j, 0)),
                  pl.BlockSpec((1, d), lambda i, j: (0, 0))],
        out_specs=pl.BlockSpec((tm, d), lambda i, j: (i, 0)),
        out_shape=jax.ShapeDtypeStruct((m, d), jnp.float32),
        scratch_shapes=[pltpu.VMEM((tm, d), MXU_DTYPE)],
        compiler_params=_params("parallel", "arbitrary"),
    )(h, g.reshape(1, d), w_up, w_down, g_final.reshape(1, d))


def _attn_kernel(q_ref, qall_ref, k_ref, v_ref, slope_ref, lam_ref, g_ref, o_ref,
                 s0_ref, s1_ref, m_ref, mfin0_ref, mfin1_ref, acc_ref, bias_ref, vaug_ref,
                 qq_ref, win_ref, *, tq, tk, unroll, lam_init):
    t = k_ref.shape[0]
    nk, nq = t // tk, t // tq
    rows = 2 * tq
    bi, qi = pl.program_id(1), pl.program_id(2)
    q0 = qi * tq
    nt = (((1,), (1,)), ((), ()))
    c = slope_ref[...][:, :1] * LOG2E
    s_refs, mfin_refs = (s0_ref, s1_ref), (mfin0_ref, mfin1_ref)

    strip_rows, strip_w = bias_ref.shape
    strip_origin = t + tq - strip_rows

    @pl.when((bi == 0) & (qi == 0))
    def _():
        r = lax.broadcasted_iota(jnp.int32, (strip_rows, HEAD_W), 0)
        u = lax.broadcasted_iota(jnp.int32, (strip_rows, HEAD_W), 1)
        rel = (r - u + strip_origin).astype(jnp.float32)

        def fill(a, _):
            u0 = pl.multiple_of(a * HEAD_W, HEAD_W)
            bias_ref[:, pl.ds(u0, HEAD_W)] = -c * jnp.abs(rel - u0.astype(jnp.float32))
            return 0

        lax.fori_loop(0, strip_w // HEAD_W, fill, 0)

    @pl.when(qi == 0)
    def _():
        vaug_ref[:, :HEAD_W] = v_ref[...]
        vaug_ref[:, HEAD_W:] = jnp.ones((t, HEAD_W), vaug_ref.dtype)

        def max_sq_norm(ref):
            x = ref[...].astype(jnp.float32)
            return jnp.max(jnp.sum(x * x, axis=1, keepdims=True), axis=0, keepdims=True)

        bound = jnp.sqrt(max_sq_norm(qall_ref) * max_sq_norm(k_ref)) * (
            DA ** -0.5 * LOG2E * 1.01)
        dist = jnp.minimum((2.0 * bound + ATTN_UNDERFLOW) / c, 2.0 * t)
        n_win = jnp.floor((2.0 * dist + (tq - 1)) / tk) + 2.0
        n_win = jnp.minimum(jnp.ceil(n_win / unroll) * unroll, 1.0 * nk)
        win_ref[0] = n_win[0, 0].astype(jnp.int32)
        win_ref[1] = jnp.floor(dist)[0, 0].astype(jnp.int32)

    n_win, dist = win_ref[0], win_ref[1]

    def first_tile(start):
        return jnp.clip(jnp.maximum(start - dist, 0) // tk, 0, nk - n_win)

    lo_cur = first_tile(q0)
    lo_prev = first_tile(q0 - tq)

    def sweep1(cur, r):
        k0 = pl.multiple_of((lo_cur + r) * tk, tk)
        s = lax.dot_general(qq_ref[...], k_ref[pl.ds(k0, tk), :], nt,
                            preferred_element_type=jnp.float32)
        bias = [bias_ref[:, pl.ds(pl.multiple_of(k0 - q0 - rb * strip_rows + strip_origin,
                                                 HEAD_W), tk)]
                for rb in range(tq // strip_rows)]
        st = s + jnp.concatenate(bias + bias, axis=0)
        s_refs[cur][:, pl.ds(pl.multiple_of(r * tk, tk), tk)] = st
        part = functools.reduce(
            jnp.maximum, [st[:, a * HEAD_W:(a + 1) * HEAD_W] for a in range(tk // HEAD_W)])
        m_ref[...] = jnp.maximum(m_ref[...], part)

    def sweep2(prev, r):
        k0 = pl.multiple_of((lo_prev + r) * tk, tk)
        cols = pl.ds(pl.multiple_of(r * tk, tk), tk)
        for mp in range(2):
            rs = slice(mp * tq, (mp + 1) * tq)
            m_rep = mfin_refs[prev][rs, :]
            p = jnp.exp2(s_refs[prev][rs, cols] - jnp.concatenate([m_rep] * (tk // HEAD_W), axis=1))
            acc_ref[mp] += jnp.dot(p.astype(vaug_ref.dtype), vaug_ref[pl.ds(k0, tk), :],
                                   preferred_element_type=jnp.float32)

    def start_sweep1():
        q = q_ref[...].astype(jnp.float32) * (DA ** -0.5 * LOG2E)
        lane = lax.broadcasted_iota(jnp.int32, q.shape, 1)
        qq_ref[...] = jnp.concatenate([jnp.where(lane < DA, q, 0.0),
                                       jnp.where(lane >= DA, q, 0.0)],
                                      axis=0).astype(qq_ref.dtype)
        m_ref[...] = jnp.full(m_ref.shape, -jnp.inf, jnp.float32)

    def end_sweep1(cur):
        m = jnp.max(m_ref[...], axis=1, keepdims=True)
        mfin_refs[cur][...] = jnp.broadcast_to(m, (rows, HEAD_W))

    def start_sweep2():
        acc_ref[...] = jnp.zeros(acc_ref.shape, jnp.float32)

    def end_sweep2():
        lp = lam_ref[...]
        lam = (jnp.exp(jnp.sum(lp[0:1] * lp[1:2], axis=1, keepdims=True))
               - jnp.exp(jnp.sum(lp[2:3] * lp[3:4], axis=1, keepdims=True)) + lam_init)
        o = [acc_ref[mp][:, :HEAD_W] / acc_ref[mp][:, HEAD_W:] for mp in range(2)]
        o = o[0] - lam * o[1]
        o_ref[...] = (_rms(o, g_ref[...]) * (1.0 - lam_init)).astype(o_ref.dtype)

    def loop(*parts):
        def body(i, _):
            for part in parts:
                for u in range(unroll):
                    part(i * unroll + u)
            return 0
        lax.fori_loop(0, n_win // unroll, body, 0)

    @pl.when(qi == 0)
    def _():
        start_sweep1()
        loop(functools.partial(sweep1, 0))
        end_sweep1(0)

    for cur in range(2):
        @pl.when((qi > 0) & (qi < nq) & (qi % 2 == cur))
        def _():
            start_sweep1()
            start_sweep2()
            loop(functools.partial(sweep1, cur), functools.partial(sweep2, 1 - cur))
            end_sweep1(cur)
            end_sweep2()

    @pl.when(qi == nq)
    def _():
        start_sweep2()
        loop(functools.partial(sweep2, (nq - 1) % 2))
        end_sweep2()


def _attention(proj, slopes, lam_p, subln_g, lam_init):
    b, t, _ = proj.shape
    tq, tk = _tile(t, ATTN_TQ), _tile(t, ATTN_TK)
    nq = t // tq
    strip_rows = min(tq, HEAD_W)
    h = N_HEADS
    return pl.pallas_call(
        functools.partial(_attn_kernel, tq=tq, tk=tk, lam_init=lam_init,
                          unroll=ATTN_UNROLL if (t // tk) % ATTN_UNROLL == 0 else 1),
        grid=(h, b, nq + 1),
        in_specs=[pl.BlockSpec((None, tq, HEAD_W),
                               lambda hi, bi, qi: (bi, jnp.minimum(qi, nq - 1), hi)),
                  pl.BlockSpec((None, t, HEAD_W), lambda hi, bi, qi: (bi, 0, hi)),
                  pl.BlockSpec((None, t, HEAD_W), lambda hi, bi, qi: (bi, 0, h + hi)),
                  pl.BlockSpec((None, t, HEAD_W), lambda hi, bi, qi: (bi, 0, 2 * h + hi)),
                  pl.BlockSpec((None, 1, HEAD_W), lambda hi, bi, qi: (hi, 0, 0)),
                  pl.BlockSpec(lam_p.shape, lambda hi, bi, qi: (0, 0)),
                  pl.BlockSpec((1, HEAD_W), lambda hi, bi, qi: (0, 0))],
        out_specs=pl.BlockSpec((None, tq, HEAD_W),
                               lambda hi, bi, qi: (bi, jnp.maximum(qi - 1, 0), hi)),
        out_shape=jax.ShapeDtypeStruct((b, t, h * HEAD_W), MXU_DTYPE),
        scratch_shapes=[pltpu.VMEM((2 * tq, t), jnp.float32),
                        pltpu.VMEM((2 * tq, t), jnp.float32),
                        pltpu.VMEM((2 * tq, HEAD_W), jnp.float32),
                        pltpu.VMEM((2 * tq, HEAD_W), jnp.float32),
                        pltpu.VMEM((2 * tq, HEAD_W), jnp.float32),
                        pltpu.VMEM((2, tq, 2 * HEAD_W), jnp.float32),
                        pltpu.VMEM((strip_rows, 2 * t + tq - strip_rows), jnp.float32),
                        pltpu.VMEM((t, 2 * HEAD_W), MXU_DTYPE),
                        pltpu.VMEM((2 * tq, HEAD_W), MXU_DTYPE),
                        pltpu.SMEM((2,), jnp.int32)],
        compiler_params=_params("parallel", "arbitrary", "arbitrary"),
    )(proj, proj, proj, proj, slopes, lam_p, subln_g.reshape(1, HEAD_W))


def _hgrn_constants(c):
    levels = []
    size = 2
    while size <= c:
        levels.append(size)
        size *= 2
    expo = np.zeros((len(levels) + 1, c, c), np.float32)
    mask = np.zeros((len(levels) + 1, c, c), np.float32)
    for li, size in enumerate(levels):
        half = size // 2
        for t in range(c):
            pos = t % size
            bd = t - pos + half - 1
            if pos >= half:
                expo[li, t, bd + 1:t + 1] = 1.0
                mask[li, t, t - pos:t - pos + half] = 1.0
            else:
                expo[li, t, t + 1:bd + 1] = 1.0
    expo[-1] = np.tril(np.ones((c, c), np.float32))
    mask[-1] = np.eye(c, dtype=np.float32)

    def pack(e):
        e = e.reshape(-1, c)
        return np.concatenate([e, e, e], axis=1)

    return (np.stack([pack(expo), pack(expo[:, ::-1, ::-1])]),
            np.stack([mask, mask[:, ::-1, ::-1]]), len(levels))


def _split3(x):
    hi = x.astype(jnp.bfloat16)
    r1 = x - hi.astype(jnp.float32)
    mid = r1.astype(jnp.bfloat16)
    lo = (r1 - mid.astype(jnp.float32)).astype(jnp.bfloat16)
    return jnp.concatenate([hi, mid, lo], axis=0)


def _hgrn_kernel(q_ref, v_ref, gate_ref, zf_ref, zb_ref, lbf_ref, lbb_ref, gn_ref,
                 expo_ref, mask_ref, o_ref, of_ref, ob_ref, sf_ref, sb_ref,
                 *, chunk, group, n_levels, out_rows):
    t = q_ref.shape[0]
    c = chunk
    n_chunks = t // c
    nt = (((1,), (1,)), ((), ()))
    tn = (((0,), (0,)), ((), ()))
    sf_ref[...] = jnp.zeros_like(sf_ref)
    sb_ref[...] = jnp.zeros_like(sb_ref)

    def gates(r0, z_ref, lb_ref):
        z = z_ref[pl.ds(r0, c), :]
        lb = lb_ref[...]
        e = jnp.exp(-jnp.abs(z))
        r = 1.0 / (1.0 + e)
        er = e * r
        pos = z >= 0.0
        f = lb + (1.0 - lb) * jnp.where(pos, r, er)
        kk = (1.0 - lb) * jnp.where(pos, er, r)
        return kk, jnp.log(f)

    def exponents(gs, d):
        both = jnp.dot(expo_ref[d], jnp.concatenate([_split3(g) for g in gs], axis=1),
                       preferred_element_type=jnp.float32)
        return [both[:, i * HEAD_W:(i + 1) * HEAD_W] for i in range(len(gs))]

    def chunk_local(r0, kk, ex_all, d):
        rows = pl.ds(r0, c)
        b = ex_all[n_levels * c:]
        ex_all = jnp.exp(ex_all)
        qf = q_ref[rows, :].astype(jnp.float32)
        v = v_ref[rows, :]
        att = mask_ref[d, n_levels] * lax.dot_general(
            qf.astype(MXU_DTYPE), kk.astype(MXU_DTYPE), nt,
            preferred_element_type=jnp.float32)
        for li in range(n_levels):
            ex = ex_all[li * c:(li + 1) * c]
            att += mask_ref[d, li] * lax.dot_general(
                (qf * ex).astype(MXU_DTYPE), (kk * ex).astype(MXU_DTYPE), nt,
                preferred_element_type=jnp.float32)
        intra = jnp.dot(att.astype(MXU_DTYPE), v, preferred_element_type=jnp.float32)
        q_dec = (qf * ex_all[n_levels * c:]).astype(MXU_DTYPE)
        b_last = b[c - 1:c] if d == 0 else b[0:1]
        kd = (kk * jnp.exp(b_last - b)).astype(MXU_DTYPE)
        update = lax.dot_general(v, kd, tn, preferred_element_type=jnp.float32)
        return rows, intra, q_dec, jnp.exp(b_last), update

    def direction(starts, z_ref, lb_ref, st_ref, out_ref, d):
        kks, gs = zip(*[gates(r0, z_ref, lb_ref) for r0 in starts])
        exs = []
        for i in range(0, len(starts), 2):
            exs += exponents(gs[i:i + 2], d)
        local = [chunk_local(r0, kk, ex, d) for r0, kk, ex in zip(starts, kks, exs)]
        st = st_ref[...]
        for rows, intra, q_dec, decay, update in local:
            out_ref[rows, :] = intra + lax.dot_general(
                q_dec, st.astype(MXU_DTYPE), nt, preferred_element_type=jnp.float32)
            st = decay * st + update
        st_ref[...] = st

    def body(n, _):
        fwd = [pl.multiple_of((n * group + i) * c, c) for i in range(group)]
        bwd = [pl.multiple_of((n_chunks - 1 - n * group - i) * c, c) for i in range(group)]
        direction(fwd, zf_ref, lbf_ref, sf_ref, of_ref, 0)
        direction(bwd, zb_ref, lbb_ref, sb_ref, ob_ref, 1)
        return 0

    lax.fori_loop(0, n_chunks // group, body, 0)

    def finish(i, _):
        rows = pl.ds(pl.multiple_of(i * out_rows, out_rows), out_rows)
        o = _rms(of_ref[rows, :] + ob_ref[rows, :], gn_ref[...])
        gate = gate_ref[rows, :].astype(jnp.float32)
        o_ref[rows, :] = (o * (gate / (1.0 + jnp.exp(-gate)))).astype(o_ref.dtype)
        return 0

    lax.fori_loop(0, t // out_rows, finish, 0)


def _hgrn(proj, zgates, lb, gn):
    b, t, _ = proj.shape
    h = N_HEADS
    c = _tile(t, HG_CHUNK)
    expo, mask, n_levels = _hgrn_constants(c)
    expo = jnp.asarray(expo, MXU_DTYPE)
    mask = jnp.asarray(mask, jnp.float32)
    out_rows = _tile(t, 512)
    group = HG_GROUP if (t // c) % HG_GROUP == 0 else 1
    head_spec = lambda off: pl.BlockSpec((None, t, HEAD_W), lambda bi, hi: (bi, 0, off + hi))
    vec_spec = pl.BlockSpec((None, 1, HEAD_W), lambda bi, hi: (hi, 0, 0))
    return pl.pallas_call(
        functools.partial(_hgrn_kernel, chunk=c, group=group, n_levels=n_levels,
                          out_rows=out_rows),
        grid=(b, h),
        in_specs=[head_spec(3 * h), head_spec(4 * h), head_spec(5 * h),
                  head_spec(0), head_spec(h), vec_spec, vec_spec, vec_spec,
                  pl.BlockSpec(expo.shape, lambda bi, hi: (0, 0, 0)),
                  pl.BlockSpec(mask.shape, lambda bi, hi: (0, 0, 0, 0))],
        out_specs=pl.BlockSpec((None, t, HEAD_W), lambda bi, hi: (bi, 0, hi)),
        out_shape=jax.ShapeDtypeStruct((b, t, h * HEAD_W), MXU_DTYPE),
        scratch_shapes=[pltpu.VMEM((t, HEAD_W), jnp.float32),
                        pltpu.VMEM((t, HEAD_W), jnp.float32),
                        pltpu.VMEM((HEAD_W, HEAD_W), jnp.float32),
                        pltpu.VMEM((HEAD_W, HEAD_W), jnp.float32)],
        compiler_params=_params("parallel", "arbitrary"),
    )(proj, proj, proj, zgates, zgates,
      lb[0].reshape(h, 1, HEAD_W), lb[1].reshape(h, 1, HEAD_W), gn.reshape(h, 1, HEAD_W),
      expo, mask)


def _conv_inproj_kernel(x_ref, g_ref, w_ref, bg_ref, u_ref, hn_ref, *, cc):
    @pl.when(pl.program_id(1) == 0)
    def _():
        hn_ref[...] = _rms(x_ref[...], g_ref[...]).astype(hn_ref.dtype)

    r = jnp.dot(hn_ref[...], w_ref[...], preferred_element_type=jnp.float32)
    bg_ref[...] = r[:, :cc].astype(bg_ref.dtype)
    u_ref[...] = (r[:, cc:2 * cc] * r[:, 2 * cc:]).astype(u_ref.dtype)


def _conv_inproj(x, g, w, tm=1024, cc=512):
    m, k = x.shape
    d = w.shape[1] // 3
    tm, cc = _tile(m, tm), _tile(d, cc)
    nb = d // cc
    w = w.reshape(k, 3, nb, cc).transpose(0, 2, 1, 3).reshape(k, 3 * d).astype(MXU_DTYPE)
    return pl.pallas_call(
        functools.partial(_conv_inproj_kernel, cc=cc),
        grid=(m // tm, nb),
        in_specs=[pl.BlockSpec((tm, k), lambda i, j: (i, 0)),
                  pl.BlockSpec((1, k), lambda i, j: (0, 0)),
                  pl.BlockSpec((k, 3 * cc), lambda i, j: (0, j))],
        out_specs=[pl.BlockSpec((tm, cc), lambda i, j: (i, j)),
                   pl.BlockSpec((tm, cc), lambda i, j: (i, j))],
        out_shape=[jax.ShapeDtypeStruct((m, d), MXU_DTYPE),
                   jax.ShapeDtypeStruct((m, d), MXU_DTYPE)],
        scratch_shapes=[pltpu.VMEM((tm, k), MXU_DTYPE)],
        compiler_params=_params("parallel", "arbitrary"),
    )(x, g.reshape(1, k), w)


def _conv_kernel(bg_ref, u_ref, up_ref, un_ref, cw_ref, w_ref, r_ref, o_ref, us_ref,
                 *, tm, cw, tiles_per_seq):
    i = pl.program_id(0)
    first = (i % tiles_per_seq) == 0
    last = (i % tiles_per_seq) == tiles_per_seq - 1
    d = u_ref.shape[1]
    ys = []
    for n in range(d // cw):
        cs = slice(n * cw, (n + 1) * cw)
        u = u_ref[:, cs].astype(jnp.float32)
        us_ref[n, 8:tm + 8, :] = u
        us_ref[n, 7:8, :] = jnp.where(
            first, 0.0, up_ref[CONV_HALO - 1:CONV_HALO, cs].astype(jnp.float32))
        us_ref[n, tm + 8:tm + 9, :] = jnp.where(
            last, 0.0, un_ref[0:1, cs].astype(jnp.float32))
        w = cw_ref[:, cs]
        conv = (w[0:1] * us_ref[n, 7:tm + 7, :] + w[1:2] * u
                + w[2:3] * us_ref[n, 9:tm + 9, :])
        ys.append((bg_ref[:, cs].astype(jnp.float32) * conv).astype(w_ref.dtype))
    o_ref[...] = r_ref[...] + jnp.dot(jnp.concatenate(ys, axis=1), w_ref[...],
                                      preferred_element_type=jnp.float32)


def _conv_mixer(bg, u, conv_w, w_out, res, seq, tm=512, cw=256):
    m, d = u.shape
    n = w_out.shape[1]
    tm, tn, cw = _tile(seq, tm), n, _tile(d, cw)
    tiles_per_seq = seq // tm
    hb = tm // CONV_HALO
    n_hb = m // CONV_HALO
    main = pl.BlockSpec((tm, d), lambda i, j: (i, 0))
    prev = pl.BlockSpec((CONV_HALO, d), lambda i, j: (jnp.maximum(i * hb - 1, 0), 0))
    nxt = pl.BlockSpec((CONV_HALO, d), lambda i, j: (jnp.minimum((i + 1) * hb, n_hb - 1), 0))
    return pl.pallas_call(
        functools.partial(_conv_kernel, tm=tm, cw=cw, tiles_per_seq=tiles_per_seq),
        grid=(m // tm, n // tn),
        in_specs=[main, main, prev, nxt,
                  pl.BlockSpec(conv_w.shape, lambda i, j: (0, 0)),
                  pl.BlockSpec((d, tn), lambda i, j: (0, j)),
                  pl.BlockSpec((tm, tn), lambda i, j: (i, j))],
        out_specs=pl.BlockSpec((tm, tn), lambda i, j: (i, j)),
        out_shape=jax.ShapeDtypeStruct((m, n), jnp.float32),
        scratch_shapes=[pltpu.VMEM((d // cw, tm + 16, cw), jnp.float32)],
        compiler_params=_params("parallel", "arbitrary"),
    )(bg, u, u, u, conv_w, w_out, res)


def kernel(x, mix_norm_g, mlp_norm_g, final_norm_g, ab_w_in, ab_w_out, diff_lambda,
           diff_subln_g, hgrn_lb, hgrn_norm_g, conv_w_in, conv_w, conv_w_out,
           mlp_w_up, mlp_w_down):
    bsz, seq, d = x.shape
    m = bsz * seq
    h = N_HEADS
    w_attn = h * HEAD_W
    n_bf = 6 * w_attn
    cast = lambda w: w.astype(MXU_DTYPE)

    hres = x.reshape(m, d)

    proj, zgates = _norm_matmul_split(hres, mix_norm_g[0], cast(ab_w_in[0]), n_bf,
                                      MXU_DTYPE, jnp.float32)
    proj = proj.reshape(bsz, seq, n_bf)
    zgates = zgates.reshape(bsz, seq, 2 * w_attn)

    lam_init = 0.8 - 0.6 * math.exp(-0.3 * 0)
    slopes = 2.0 ** (-8.0 * jnp.arange(1, h + 1, dtype=jnp.float32) / h)
    slopes = jnp.broadcast_to(slopes[:, None, None], (h, 1, HEAD_W))
    oa = _attention(proj, slopes, diff_lambda[0].astype(jnp.float32), diff_subln_g[0], lam_init)

    lb = jnp.cumsum(jax.nn.softmax(hgrn_lb.astype(jnp.float32), axis=1), axis=1)[:, 0]
    ob = _hgrn(proj, zgates, lb, hgrn_norm_g[0])

    hres = _proj2_residual(oa.reshape(m, w_attn), ob.reshape(m, w_attn),
                           cast(ab_w_out[0]), hres)
    hres = _mlp(hres, mlp_norm_g[0], cast(mlp_w_up[0]), cast(mlp_w_down[0]),
                final_norm_g, False)

    bgate, u = _conv_inproj(hres, mix_norm_g[1], conv_w_in[0])
    hres = _conv_mixer(bgate, u, conv_w[0], cast(conv_w_out[0]), hres, seq)
    hres = _mlp(hres, mlp_norm_g[1], cast(mlp_w_up[1]), cast(mlp_w_down[1]),
                final_norm_g, True)
    return hres.reshape(bsz, seq, d)
```

```python
import functools
import math

import numpy as np
import jax
import jax.numpy as jnp
from jax import lax
from jax.experimental import pallas as pl
from jax.experimental.pallas import tpu as pltpu

EPS = 1e-6
N_HEADS = 8
HEAD_W = 128
DA = 64
HG_CHUNK = 64
HG_GROUP = 8
CONV_HALO = 16
MXU_DTYPE = jnp.bfloat16
LOG2E = 1.4426950408889634
ATTN_TQ = 256
ATTN_TK = 512
ATTN_UNROLL = 4
ATTN_UNDERFLOW = 160.0
VMEM_LIMIT = 56 * 1024 * 1024


def _tile(dim, pref):
    return pref if dim % pref == 0 else dim


def _rms(x, g):
    ms = jnp.mean(x * x, axis=-1, keepdims=True)
    return x * lax.rsqrt(ms + EPS) * g


def _params(*sem):
    return pltpu.CompilerParams(dimension_semantics=sem, vmem_limit_bytes=VMEM_LIMIT)


def _norm_matmul_split_kernel(x_ref, g_ref, w_ref, lo_ref, hi_ref, hn_ref, *, n_lo):
    j = pl.program_id(1)

    @pl.when(j == 0)
    def _():
        hn_ref[...] = _rms(x_ref[...], g_ref[...]).astype(hn_ref.dtype)

    def product():
        return jnp.dot(hn_ref[...], w_ref[...], preferred_element_type=jnp.float32)

    @pl.when(j < n_lo)
    def _():
        lo_ref[...] = product().astype(lo_ref.dtype)

    @pl.when(j >= n_lo)
    def _():
        hi_ref[...] = product().astype(hi_ref.dtype)


def _norm_matmul_split(x, g, w, n_split, lo_dtype, hi_dtype, tm=1024, tn=1024):
    m, k = x.shape
    n = w.shape[1]
    tm, tn = _tile(m, tm), _tile(n_split, tn)
    assert n_split % tn == 0 and (n - n_split) % tn == 0
    n_lo = n_split // tn
    return pl.pallas_call(
        functools.partial(_norm_matmul_split_kernel, n_lo=n_lo),
        grid=(m // tm, n // tn),
        in_specs=[pl.BlockSpec((tm, k), lambda i, j: (i, 0)),
                  pl.BlockSpec((1, k), lambda i, j: (0, 0)),
                  pl.BlockSpec((k, tn), lambda i, j: (0, j))],
        out_specs=[pl.BlockSpec((tm, tn), lambda i, j: (i, jnp.minimum(j, n_lo - 1))),
                   pl.BlockSpec((tm, tn), lambda i, j: (i, jnp.maximum(j - n_lo, 0)))],
        out_shape=[jax.ShapeDtypeStruct((m, n_split), lo_dtype),
                   jax.ShapeDtypeStruct((m, n - n_split), hi_dtype)],
        scratch_shapes=[pltpu.VMEM((tm, k), MXU_DTYPE)],
        compiler_params=_params("parallel", "arbitrary"),
    )(x, g.reshape(1, k), w)


def _proj2_kernel(a_ref, b_ref, wa_ref, wb_ref, r_ref, o_ref):
    acc = jnp.dot(a_ref[...], wa_ref[...], preferred_element_type=jnp.float32)
    acc += jnp.dot(b_ref[...], wb_ref[...], preferred_element_type=jnp.float32)
    o_ref[...] = r_ref[...] + acc


def _proj2_residual(a, b, w, res, tm=512, tn=2048):
    m, ka = a.shape
    kb = b.shape[1]
    n = w.shape[1]
    assert ka == kb and w.shape[0] == ka + kb
    tm, tn = _tile(m, tm), _tile(n, tn)
    return pl.pallas_call(
        _proj2_kernel,
        grid=(m // tm, n // tn),
        in_specs=[pl.BlockSpec((tm, ka), lambda i, j: (i, 0)),
                  pl.BlockSpec((tm, kb), lambda i, j: (i, 0)),
                  pl.BlockSpec((ka, tn), lambda i, j: (0, j)),
                  pl.BlockSpec((kb, tn), lambda i, j: (1, j)),
                  pl.BlockSpec((tm, tn), lambda i, j: (i, j))],
        out_specs=pl.BlockSpec((tm, tn), lambda i, j: (i, j)),
        out_shape=jax.ShapeDtypeStruct((m, n), jnp.float32),
        compiler_params=_params("parallel", "arbitrary"),
    )(a, b, w, w, res)


def _mlp_kernel(h_ref, g_ref, wu_ref, wd_ref, gf_ref, o_ref, hn_ref, *, final_norm):
    j = pl.program_id(1)

    @pl.when(j == 0)
    def _():
        h = h_ref[...]
        hn_ref[...] = _rms(h, g_ref[...]).astype(hn_ref.dtype)
        o_ref[...] = h

    a = jnp.dot(hn_ref[...], wu_ref[...], preferred_element_type=jnp.float32)
    a = jnp.maximum(a, 0.0)
    a = (a * a).astype(wd_ref.dtype)
    o_ref[...] += jnp.dot(a, wd_ref[...], preferred_element_type=jnp.float32)

    if final_norm:
        @pl.when(j == pl.num_programs(1) - 1)
        def _():
            o_ref[...] = _rms(o_ref[...], gf_ref[...])


def _mlp(h, g, w_up, w_down, g_final, final_norm, tm=1024, tf=512):
    m, d = h.shape
    f = w_up.shape[1]
    tm, tf = _tile(m, tm), _tile(f, tf)
    return pl.pallas_call(
        functools.partial(_mlp_kernel, final_norm=final_norm),
        grid=(m // tm, f // tf),
        in_specs=[pl.BlockSpec((tm, d), lambda i, j: (i, 0)),
                  pl.BlockSpec((1, d), lambda i, j: (0, 0)),
                  pl.BlockSpec((d, tf), lambda i, j: (0, j)),
                  pl.BlockSpec((tf, d), lambda i, j: (j, 0)),
                  pl.BlockSpec((1, d), lambda i, j: (0, 0))],
        out_specs=pl.BlockSpec((tm, d), lambda i, j: (i, 0)),
        out_shape=jax.ShapeDtypeStruct((m, d), jnp.float32),
        scratch_shapes=[pltpu.VMEM((tm, d), MXU_DTYPE)],
        compiler_params=_params("parallel", "arbitrary"),
    )(h, g.reshape(1, d), w_up, w_down, g_final.reshape(1, d))


def _attn_kernel(q_ref, qall_ref, k_ref, v_ref, slope_ref, lam_ref, g_ref, o_ref,
                 s0_ref, s1_ref, m_ref, mfin0_ref, mfin1_ref, acc_ref, bias_ref, vaug_ref,
                 qq_ref, win_ref, *, tq, tk, unroll, lam_init):
    t = k_ref.shape[0]
    nk, nq = t // tk, t // tq
    rows = 2 * tq
    bi, qi = pl.program_id(1), pl.program_id(2)
    q0 = qi * tq
    nt = (((1,), (1,)), ((), ()))
    c = slope_ref[...][:, :1] * LOG2E
    s_refs, mfin_refs = (s0_ref, s1_ref), (mfin0_ref, mfin1_ref)

    strip_rows, strip_w = bias_ref.shape
    strip_origin = t + tq - strip_rows

    @pl.when((bi == 0) & (qi == 0))
    def _():
        r = lax.broadcasted_iota(jnp.int32, (strip_rows, HEAD_W), 0)
        u = lax.broadcasted_iota(jnp.int32, (strip_rows, HEAD_W), 1)
        rel = (r - u + strip_origin).astype(jnp.float32)

        def fill(a, _):
            u0 = pl.multiple_of(a * HEAD_W, HEAD_W)
            bias_ref[:, pl.ds(u0, HEAD_W)] = -c * jnp.abs(rel - u0.astype(jnp.float32))
            return 0

        lax.fori_loop(0, strip_w // HEAD_W, fill, 0)

    @pl.when(qi == 0)
    def _():
        vaug_ref[:, :HEAD_W] = v_ref[...]
        vaug_ref[:, HEAD_W:] = jnp.ones((t, HEAD_W), vaug_ref.dtype)

        def max_sq_norm(ref):
            x = ref[...].astype(jnp.float32)
            return jnp.max(jnp.sum(x * x, axis=1, keepdims=True), axis=0, keepdims=True)

        bound = jnp.sqrt(max_sq_norm(qall_ref) * max_sq_norm(k_ref)) * (
            DA ** -0.5 * LOG2E * 1.01)
        dist = jnp.minimum((2.0 * bound + ATTN_UNDERFLOW) / c, 2.0 * t)
        n_win = jnp.floor((2.0 * dist + (tq - 1)) / tk) + 2.0
        n_win = jnp.minimum(jnp.ceil(n_win / unroll) * unroll, 1.0 * nk)
        win_ref[0] = n_win[0, 0].astype(jnp.int32)
        win_ref[1] = jnp.floor(dist)[0, 0].astype(jnp.int32)

    n_win, dist = win_ref[0], win_ref[1]

    def first_tile(start):
        return jnp.clip(jnp.maximum(start - dist, 0) // tk, 0, nk - n_win)

    lo_cur = first_tile(q0)
    lo_prev = first_tile(q0 - tq)

    def sweep1(cur, r):
        k0 = pl.multiple_of((lo_cur + r) * tk, tk)
        s = lax.dot_general(qq_ref[...], k_ref[pl.ds(k0, tk), :], nt,
                            preferred_element_type=jnp.float32)
        bias = [bias_ref[:, pl.ds(pl.multiple_of(k0 - q0 - rb * strip_rows + strip_origin,
                                                 HEAD_W), tk)]
                for rb in range(tq // strip_rows)]
        st = s + jnp.concatenate(bias + bias, axis=0)
        s_refs[cur][:, r * tk:(r + 1) * tk] = st
        part = functools.reduce(
            jnp.maximum, [st[:, a * HEAD_W:(a + 1) * HEAD_W] for a in range(tk // HEAD_W)])
        m_ref[...] = jnp.maximum(m_ref[...], part)

    def sweep2(prev, r):
        k0 = pl.multiple_of((lo_prev + r) * tk, tk)
        cols = slice(r * tk, (r + 1) * tk)
        for mp in range(2):
            rs = slice(mp * tq, (mp + 1) * tq)
            m_rep = mfin_refs[prev][rs, :]
            p = jnp.exp2(s_refs[prev][rs, cols] - jnp.concatenate([m_rep] * (tk // HEAD_W), axis=1))
            acc_ref[mp] += jnp.dot(p.astype(vaug_ref.dtype), vaug_ref[pl.ds(k0, tk), :],
                                   preferred_element_type=jnp.float32)

    def start_sweep1():
        q = q_ref[...].astype(jnp.float32) * (DA ** -0.5 * LOG2E)
        lane = lax.broadcasted_iota(jnp.int32, q.shape, 1)
        qq_ref[...] = jnp.concatenate([jnp.where(lane < DA, q, 0.0),
                                       jnp.where(lane >= DA, q, 0.0)],
                                      axis=0).astype(qq_ref.dtype)
        m_ref[...] = jnp.full(m_ref.shape, -jnp.inf, jnp.float32)

    def end_sweep1(cur):
        m = jnp.max(m_ref[...], axis=1, keepdims=True)
        mfin_refs[cur][...] = jnp.broadcast_to(m, (rows, HEAD_W))

    def start_sweep2():
        acc_ref[...] = jnp.zeros(acc_ref.shape, jnp.float32)

    def end_sweep2():
        lp = lam_ref[...]
        lam = (jnp.exp(jnp.sum(lp[0:1] * lp[1:2], axis=1, keepdims=True))
               - jnp.exp(jnp.sum(lp[2:3] * lp[3:4], axis=1, keepdims=True)) + lam_init)
        o = [acc_ref[mp][:, :HEAD_W] / acc_ref[mp][:, HEAD_W:] for mp in range(2)]
        o = o[0] - lam * o[1]
        o_ref[...] = (_rms(o, g_ref[...]) * (1.0 - lam_init)).astype(o_ref.dtype)

    def loop(*parts):
        for n in range(unroll, nk + 1, unroll):
            @pl.when(n_win == n)
            def _():
                for part in parts:
                    for r in range(n):
                        part(r)

    @pl.when(qi == 0)
    def _():
        start_sweep1()
        loop(functools.partial(sweep1, 0))
        end_sweep1(0)

    for cur in range(2):
        @pl.when((qi > 0) & (qi < nq) & (qi % 2 == cur))
        def _():
            start_sweep1()
            start_sweep2()
            loop(functools.partial(sweep1, cur), functools.partial(sweep2, 1 - cur))
            end_sweep1(cur)
            end_sweep2()

    @pl.when(qi == nq)
    def _():
        start_sweep2()
        loop(functools.partial(sweep2, (nq - 1) % 2))
        end_sweep2()


def _attention(proj, slopes, lam_p, subln_g, lam_init):
    b, t, _ = proj.shape
    tq, tk = _tile(t, ATTN_TQ), _tile(t, ATTN_TK)
    nq = t // tq
    strip_rows = min(tq, HEAD_W)
    h = N_HEADS
    return pl.pallas_call(
        functools.partial(_attn_kernel, tq=tq, tk=tk, lam_init=lam_init,
                          unroll=ATTN_UNROLL if (t // tk) % ATTN_UNROLL == 0 else 1),
        grid=(h, b, nq + 1),
        in_specs=[pl.BlockSpec((None, tq, HEAD_W),
                               lambda hi, bi, qi: (bi, jnp.minimum(qi, nq - 1), hi)),
                  pl.BlockSpec((None, t, HEAD_W), lambda hi, bi, qi: (bi, 0, hi)),
                  pl.BlockSpec((None, t, HEAD_W), lambda hi, bi, qi: (bi, 0, h + hi)),
                  pl.BlockSpec((None, t, HEAD_W), lambda hi, bi, qi: (bi, 0, 2 * h + hi)),
                  pl.BlockSpec((None, 1, HEAD_W), lambda hi, bi, qi: (hi, 0, 0)),
                  pl.BlockSpec(lam_p.shape, lambda hi, bi, qi: (0, 0)),
                  pl.BlockSpec((1, HEAD_W), lambda hi, bi, qi: (0, 0))],
        out_specs=pl.BlockSpec((None, tq, HEAD_W),
                               lambda hi, bi, qi: (bi, jnp.maximum(qi - 1, 0), hi)),
        out_shape=jax.ShapeDtypeStruct((b, t, h * HEAD_W), MXU_DTYPE),
        scratch_shapes=[pltpu.VMEM((2 * tq, t), jnp.float32),
                        pltpu.VMEM((2 * tq, t), jnp.float32),
                        pltpu.VMEM((2 * tq, HEAD_W), jnp.float32),
                        pltpu.VMEM((2 * tq, HEAD_W), jnp.float32),
                        pltpu.VMEM((2 * tq, HEAD_W), jnp.float32),
                        pltpu.VMEM((2, tq, 2 * HEAD_W), jnp.float32),
                        pltpu.VMEM((strip_rows, 2 * t + tq - strip_rows), jnp.float32),
                        pltpu.VMEM((t, 2 * HEAD_W), MXU_DTYPE),
                        pltpu.VMEM((2 * tq, HEAD_W), MXU_DTYPE),
                        pltpu.SMEM((2,), jnp.int32)],
        compiler_params=_params("parallel", "arbitrary", "arbitrary"),
    )(proj, proj, proj, proj, slopes, lam_p, subln_g.reshape(1, HEAD_W))


def _hgrn_constants(c):
    levels = []
    size = 2
    while size <= c:
        levels.append(size)
        size *= 2
    expo = np.zeros((len(levels) + 1, c, c), np.float32)
    mask = np.zeros((len(levels) + 1, c, c), np.float32)
    for li, size in enumerate(levels):
        half = size // 2
        for t in range(c):
            pos = t % size
            bd = t - pos + half - 1
            if pos >= half:
                expo[li, t, bd + 1:t + 1] = 1.0
                mask[li, t, t - pos:t - pos + half] = 1.0
            else:
                expo[li, t, t + 1:bd + 1] = 1.0
    expo[-1] = np.tril(np.ones((c, c), np.float32))
    mask[-1] = np.eye(c, dtype=np.float32)

    def pack(e):
        e = e.reshape(-1, c)
        return np.concatenate([e, e, e], axis=1)

    return (np.stack([pack(expo), pack(expo[:, ::-1, ::-1])]),
            np.stack([mask, mask[:, ::-1, ::-1]]), len(levels))


def _split3(x):
    hi = x.astype(jnp.bfloat16)
    r1 = x - hi.astype(jnp.float32)
    mid = r1.astype(jnp.bfloat16)
    lo = (r1 - mid.astype(jnp.float32)).astype(jnp.bfloat16)
    return jnp.concatenate([hi, mid, lo], axis=0)


def _hgrn_kernel(q_ref, v_ref, gate_ref, zf_ref, zb_ref, lbf_ref, lbb_ref, gn_ref,
                 expo_ref, mask_ref, o_ref, of_ref, ob_ref, sf_ref, sb_ref,
                 *, chunk, group, n_levels, out_rows):
    t = q_ref.shape[0]
    c = chunk
    n_chunks = t // c
    nt = (((1,), (1,)), ((), ()))
    tn = (((0,), (0,)), ((), ()))
    sf_ref[...] = jnp.zeros_like(sf_ref)
    sb_ref[...] = jnp.zeros_like(sb_ref)

    def gates(r0, z_ref, lb_ref):
        z = z_ref[pl.ds(r0, c), :]
        lb = lb_ref[...]
        e = jnp.exp(-jnp.abs(z))
        r = 1.0 / (1.0 + e)
        er = e * r
        pos = z >= 0.0
        f = lb + (1.0 - lb) * jnp.where(pos, r, er)
        kk = (1.0 - lb) * jnp.where(pos, er, r)
        return kk, jnp.log(f)

    def exponents(gs, d):
        both = jnp.dot(expo_ref[d], jnp.concatenate([_split3(g) for g in gs], axis=1),
                       preferred_element_type=jnp.float32)
        return [both[:, i * HEAD_W:(i + 1) * HEAD_W] for i in range(len(gs))]

    def chunk_local(r0, kk, ex_all, d):
        rows = pl.ds(r0, c)
        b = ex_all[n_levels * c:]
        ex_all = jnp.exp(ex_all)
        qf = q_ref[rows, :].astype(jnp.float32)
        v = v_ref[rows, :]
        att = mask_ref[d, n_levels] * lax.dot_general(
            qf.astype(MXU_DTYPE), kk.astype(MXU_DTYPE), nt,
            preferred_element_type=jnp.float32)
        for li in range(n_levels):
            ex = ex_all[li * c:(li + 1) * c]
            att += mask_ref[d, li] * lax.dot_general(
                (qf * ex).astype(MXU_DTYPE), (kk * ex).astype(MXU_DTYPE), nt,
                preferred_element_type=jnp.float32)
        intra = jnp.dot(att.astype(MXU_DTYPE), v, preferred_element_type=jnp.float32)
        q_dec = (qf * ex_all[n_levels * c:]).astype(MXU_DTYPE)
        b_last = b[c - 1:c] if d == 0 else b[0:1]
        kd = (kk * jnp.exp(b_last - b)).astype(MXU_DTYPE)
        update = lax.dot_general(v, kd, tn, preferred_element_type=jnp.float32)
        return rows, intra, q_dec, jnp.exp(b_last), update

    def direction(starts, z_ref, lb_ref, st_ref, out_ref, d):
        kks, gs = zip(*[gates(r0, z_ref, lb_ref) for r0 in starts])
        exs = []
        for i in range(0, len(starts), 2):
            exs += exponents(gs[i:i + 2], d)
        local = [chunk_local(r0, kk, ex, d) for r0, kk, ex in zip(starts, kks, exs)]
        st = st_ref[...]
        for rows, intra, q_dec, decay, update in local:
            out_ref[rows, :] = intra + lax.dot_general(
                q_dec, st.astype(MXU_DTYPE), nt, preferred_element_type=jnp.float32)
            st = decay * st + update
        st_ref[...] = st

    def body(n, _):
        fwd = [pl.multiple_of((n * group + i) * c, c) for i in range(group)]
        bwd = [pl.multiple_of((n_chunks - 1 - n * group - i) * c, c) for i in range(group)]
        direction(fwd, zf_ref, lbf_ref, sf_ref, of_ref, 0)
        direction(bwd, zb_ref, lbb_ref, sb_ref, ob_ref, 1)
        return 0

    lax.fori_loop(0, n_chunks // group, body, 0)

    def finish(i, _):
        rows = pl.ds(pl.multiple_of(i * out_rows, out_rows), out_rows)
        o = _rms(of_ref[rows, :] + ob_ref[rows, :], gn_ref[...])
        gate = gate_ref[rows, :].astype(jnp.float32)
        o_ref[rows, :] = (o * (gate / (1.0 + jnp.exp(-gate)))).astype(o_ref.dtype)
        return 0

    lax.fori_loop(0, t // out_rows, finish, 0)


def _hgrn(proj, zgates, lb, gn):
    b, t, _ = proj.shape
    h = N_HEADS
    c = _tile(t, HG_CHUNK)
    expo, mask, n_levels = _hgrn_constants(c)
    expo = jnp.asarray(expo, MXU_DTYPE)
    mask = jnp.asarray(mask, jnp.float32)
    out_rows = _tile(t, 512)
    group = HG_GROUP if (t // c) % HG_GROUP == 0 else 1
    head_spec = lambda off: pl.BlockSpec((None, t, HEAD_W), lambda bi, hi: (bi, 0, off + hi))
    vec_spec = pl.BlockSpec((None, 1, HEAD_W), lambda bi, hi: (hi, 0, 0))
    return pl.pallas_call(
        functools.partial(_hgrn_kernel, chunk=c, group=group, n_levels=n_levels,
                          out_rows=out_rows),
        grid=(b, h),
        in_specs=[head_spec(3 * h), head_spec(4 * h), head_spec(5 * h),
                  head_spec(0), head_spec(h), vec_spec, vec_spec, vec_spec,
                  pl.BlockSpec(expo.shape, lambda bi, hi: (0, 0, 0)),
                  pl.BlockSpec(mask.shape, lambda bi, hi: (0, 0, 0, 0))],
        out_specs=pl.BlockSpec((None, t, HEAD_W), lambda bi, hi: (bi, 0, hi)),
        out_shape=jax.ShapeDtypeStruct((b, t, h * HEAD_W), MXU_DTYPE),
        scratch_shapes=[pltpu.VMEM((t, HEAD_W), jnp.float32),
                        pltpu.VMEM((t, HEAD_W), jnp.float32),
                        pltpu.VMEM((HEAD_W, HEAD_W), jnp.float32),
                        pltpu.VMEM((HEAD_W, HEAD_W), jnp.float32)],
        compiler_params=_params("parallel", "arbitrary"),
    )(proj, proj, proj, zgates, zgates,
      lb[0].reshape(h, 1, HEAD_W), lb[1].reshape(h, 1, HEAD_W), gn.reshape(h, 1, HEAD_W),
      expo, mask)


def _conv_inproj_kernel(x_ref, g_ref, w_ref, bg_ref, u_ref, hn_ref, *, cc):
    @pl.when(pl.program_id(1) == 0)
    def _():
        hn_ref[...] = _rms(x_ref[...], g_ref[...]).astype(hn_ref.dtype)

    r = jnp.dot(hn_ref[...], w_ref[...], preferred_element_type=jnp.float32)
    bg_ref[...] = r[:, :cc].astype(bg_ref.dtype)
    u_ref[...] = (r[:, cc:2 * cc] * r[:, 2 * cc:]).astype(u_ref.dtype)


def _conv_inproj(x, g, w, tm=1024, cc=512):
    m, k = x.shape
    d = w.shape[1] // 3
    tm, cc = _tile(m, tm), _tile(d, cc)
    nb = d // cc
    w = w.reshape(k, 3, nb, cc).transpose(0, 2, 1, 3).reshape(k, 3 * d).astype(MXU_DTYPE)
    return pl.pallas_call(
        functools.partial(_conv_inproj_kernel, cc=cc),
        grid=(m // tm, nb),
        in_specs=[pl.BlockSpec((tm, k), lambda i, j: (i, 0)),
                  pl.BlockSpec((1, k), lambda i, j: (0, 0)),
                  pl.BlockSpec((k, 3 * cc), lambda i, j: (0, j))],
        out_specs=[pl.BlockSpec((tm, cc), lambda i, j: (i, j)),
                   pl.BlockSpec((tm, cc), lambda i, j: (i, j))],
        out_shape=[jax.ShapeDtypeStruct((m, d), MXU_DTYPE),
                   jax.ShapeDtypeStruct((m, d), MXU_DTYPE)],
        scratch_shapes=[pltpu.VMEM((tm, k), MXU_DTYPE)],
        compiler_params=_params("parallel", "arbitrary"),
    )(x, g.reshape(1, k), w)


def _conv_kernel(bg_ref, u_ref, up_ref, un_ref, cw_ref, w_ref, r_ref, o_ref, us_ref,
                 *, tm, cw, tiles_per_seq):
    i = pl.program_id(0)
    first = (i % tiles_per_seq) == 0
    last = (i % tiles_per_seq) == tiles_per_seq - 1
    d = u_ref.shape[1]
    ys = []
    for n in range(d // cw):
        cs = slice(n * cw, (n + 1) * cw)
        u = u_ref[:, cs].astype(jnp.float32)
        us_ref[n, 8:tm + 8, :] = u
        us_ref[n, 7:8, :] = jnp.where(
            first, 0.0, up_ref[CONV_HALO - 1:CONV_HALO, cs].astype(jnp.float32))
        us_ref[n, tm + 8:tm + 9, :] = jnp.where(
            last, 0.0, un_ref[0:1, cs].astype(jnp.float32))
        w = cw_ref[:, cs]
        conv = (w[0:1] * us_ref[n, 7:tm + 7, :] + w[1:2] * u
                + w[2:3] * us_ref[n, 9:tm + 9, :])
        ys.append((bg_ref[:, cs].astype(jnp.float32) * conv).astype(w_ref.dtype))
    o_ref[...] = r_ref[...] + jnp.dot(jnp.concatenate(ys, axis=1), w_ref[...],
                                      preferred_element_type=jnp.float32)


def _conv_mixer(bg, u, conv_w, w_out, res, seq, tm=512, cw=256):
    m, d = u.shape
    n = w_out.shape[1]
    tm, tn, cw = _tile(seq, tm), n, _tile(d, cw)
    tiles_per_seq = seq // tm
    hb = tm // CONV_HALO
    n_hb = m // CONV_HALO
    main = pl.BlockSpec((tm, d), lambda i, j: (i, 0))
    prev = pl.BlockSpec((CONV_HALO, d), lambda i, j: (jnp.maximum(i * hb - 1, 0), 0))
    nxt = pl.BlockSpec((CONV_HALO, d), lambda i, j: (jnp.minimum((i + 1) * hb, n_hb - 1), 0))
    return pl.pallas_call(
        functools.partial(_conv_kernel, tm=tm, cw=cw, tiles_per_seq=tiles_per_seq),
        grid=(m // tm, n // tn),
        in_specs=[main, main, prev, nxt,
                  pl.BlockSpec(conv_w.shape, lambda i, j: (0, 0)),
                  pl.BlockSpec((d, tn), lambda i, j: (0, j)),
                  pl.BlockSpec((tm, tn), lambda i, j: (i, j))],
        out_specs=pl.BlockSpec((tm, tn), lambda i, j: (i, j)),
        out_shape=jax.ShapeDtypeStruct((m, n), jnp.float32),
        scratch_shapes=[pltpu.VMEM((d // cw, tm + 16, cw), jnp.float32)],
        compiler_params=_params("parallel", "arbitrary"),
    )(bg, u, u, u, conv_w, w_out, res)


def kernel(x, mix_norm_g, mlp_norm_g, final_norm_g, ab_w_in, ab_w_out, diff_lambda,
           diff_subln_g, hgrn_lb, hgrn_norm_g, conv_w_in, conv_w, conv_w_out,
           mlp_w_up, mlp_w_down):
    bsz, seq, d = x.shape
    m = bsz * seq
    h = N_HEADS
    w_attn = h * HEAD_W
    n_bf = 6 * w_attn
    cast = lambda w: w.astype(MXU_DTYPE)

    hres = x.reshape(m, d)

    proj, zgates = _norm_matmul_split(hres, mix_norm_g[0], cast(ab_w_in[0]), n_bf,
                                      MXU_DTYPE, jnp.float32)
    proj = proj.reshape(bsz, seq, n_bf)
    zgates = zgates.reshape(bsz, seq, 2 * w_attn)

    lam_init = 0.8 - 0.6 * math.exp(-0.3 * 0)
    slopes = 2.0 ** (-8.0 * jnp.arange(1, h + 1, dtype=jnp.float32) / h)
    slopes = jnp.broadcast_to(slopes[:, None, None], (h, 1, HEAD_W))
    oa = _attention(proj, slopes, diff_lambda[0].astype(jnp.float32), diff_subln_g[0], lam_init)

    lb = jnp.cumsum(jax.nn.softmax(hgrn_lb.astype(jnp.float32), axis=1), axis=1)[:, 0]
    ob = _hgrn(proj, zgates, lb, hgrn_norm_g[0])

    hres = _proj2_residual(oa.reshape(m, w_attn), ob.reshape(m, w_attn),
                           cast(ab_w_out[0]), hres)
    hres = _mlp(hres, mlp_norm_g[0], cast(mlp_w_up[0]), cast(mlp_w_down[0]),
                final_norm_g, False)

    bgate, u = _conv_inproj(hres, mix_norm_g[1], conv_w_in[0])
    hres = _conv_mixer(bgate, u, conv_w[0], cast(conv_w_out[0]), hres, seq)
    hres = _mlp(hres, mlp_norm_g[1], cast(mlp_w_up[1]), cast(mlp_w_down[1]),
                final_norm_g, True)
    return hres.reshape(bsz, seq, d)
```

```python
import functools
import math

import numpy as np
import jax
import jax.numpy as jnp
from jax import lax
from jax.experimental import pallas as pl
from jax.experimental.pallas import tpu as pltpu

EPS = 1e-6
N_HEADS = 8
HEAD_W = 128
DA = 64
HG_CHUNK = 64
HG_GROUP = 8
CONV_HALO = 16
MXU_DTYPE = jnp.bfloat16
LOG2E = 1.4426950408889634
ATTN_TQ = 256
ATTN_TK = 512
ATTN_UNROLL = 4
ATTN_UNDERFLOW = 160.0
VMEM_LIMIT = 56 * 1024 * 1024


def _tile(dim, pref):
    return pref if dim % pref == 0 else dim


def _rms(x, g):
    ms = jnp.mean(x * x, axis=-1, keepdims=True)
    return x * lax.rsqrt(ms + EPS) * g


def _params(*sem):
    return pltpu.CompilerParams(dimension_semantics=sem, vmem_limit_bytes=VMEM_LIMIT)


def _norm_matmul_split_kernel(x_ref, g_ref, w_ref, lo_ref, hi_ref, hn_ref, *, n_lo):
    j = pl.program_id(1)

    @pl.when(j == 0)
    def _():
        hn_ref[...] = _rms(x_ref[...], g_ref[...]).astype(hn_ref.dtype)

    def product():
        return jnp.dot(hn_ref[...], w_ref[...], preferred_element_type=jnp.float32)

    @pl.when(j < n_lo)
    def _():
        lo_ref[...] = product().astype(lo_ref.dtype)

    @pl.when(j >= n_lo)
    def _():
        hi_ref[...] = product().astype(hi_ref.dtype)


def _norm_matmul_split(x, g, w, n_split, lo_dtype, hi_dtype, tm=1024, tn=1024):
    m, k = x.shape
    n = w.shape[1]
    tm, tn = _tile(m, tm), _tile(n_split, tn)
    assert n_split % tn == 0 and (n - n_split) % tn == 0
    n_lo = n_split // tn
    return pl.pallas_call(
        functools.partial(_norm_matmul_split_kernel, n_lo=n_lo),
        grid=(m // tm, n // tn),
        in_specs=[pl.BlockSpec((tm, k), lambda i, j: (i, 0)),
                  pl.BlockSpec((1, k), lambda i, j: (0, 0)),
                  pl.BlockSpec((k, tn), lambda i, j: (0, j))],
        out_specs=[pl.BlockSpec((tm, tn), lambda i, j: (i, jnp.minimum(j, n_lo - 1))),
                   pl.BlockSpec((tm, tn), lambda i, j: (i, jnp.maximum(j - n_lo, 0)))],
        out_shape=[jax.ShapeDtypeStruct((m, n_split), lo_dtype),
                   jax.ShapeDtypeStruct((m, n - n_split), hi_dtype)],
        scratch_shapes=[pltpu.VMEM((tm, k), MXU_DTYPE)],
        compiler_params=_params("parallel", "arbitrary"),
    )(x, g.reshape(1, k), w)


def _proj2_kernel(a_ref, b_ref, wa_ref, wb_ref, r_ref, o_ref):
    acc = jnp.dot(a_ref[...], wa_ref[...], preferred_element_type=jnp.float32)
    acc += jnp.dot(b_ref[...], wb_ref[...], preferred_element_type=jnp.float32)
    o_ref[...] = r_ref[...] + acc


def _proj2_residual(a, b, w, res, tm=512, tn=2048):
    m, ka = a.shape
    kb = b.shape[1]
    n = w.shape[1]
    assert ka == kb and w.shape[0] == ka + kb
    tm, tn = _tile(m, tm), _tile(n, tn)
    return pl.pallas_call(
        _proj2_kernel,
        grid=(m // tm, n // tn),
        in_specs=[pl.BlockSpec((tm, ka), lambda i, j: (i, 0)),
                  pl.BlockSpec((tm, kb), lambda i, j: (i, 0)),
                  pl.BlockSpec((ka, tn), lambda i, j: (0, j)),
                  pl.BlockSpec((kb, tn), lambda i, j: (1, j)),
                  pl.BlockSpec((tm, tn), lambda i, j: (i, j))],
        out_specs=pl.BlockSpec((tm, tn), lambda i, j: (i, j)),
        out_shape=jax.ShapeDtypeStruct((m, n), jnp.float32),
        compiler_params=_params("parallel", "arbitrary"),
    )(a, b, w, w, res)


def _mlp_kernel(h_ref, g_ref, wu_ref, wd_ref, gf_ref, o_ref, hn_ref, *, final_norm):
    j = pl.program_id(1)

    @pl.when(j == 0)
    def _():
        h = h_ref[...]
        hn_ref[...] = _rms(h, g_ref[...]).astype(hn_ref.dtype)
        o_ref[...] = h

    a = jnp.dot(hn_ref[...], wu_ref[...], preferred_element_type=jnp.float32)
    a = jnp.maximum(a, 0.0)
    a = (a * a).astype(wd_ref.dtype)
    o_ref[...] += jnp.dot(a, wd_ref[...], preferred_element_type=jnp.float32)

    if final_norm:
        @pl.when(j == pl.num_programs(1) - 1)
        def _():
            o_ref[...] = _rms(o_ref[...], gf_ref[...])


def _mlp(h, g, w_up, w_down, g_final, final_norm, tm=1024, tf=512):
    m, d = h.shape
    f = w_up.shape[1]
    tm, tf = _tile(m, tm), _tile(f, tf)
    return pl.pallas_call(
        functools.partial(_mlp_kernel, final_norm=final_norm),
        grid=(m // tm, f // tf),
        in_specs=[pl.BlockSpec((tm, d), lambda i, j: (i, 0)),
                  pl.BlockSpec((1, d), lambda i, j: (0, 0)),
                  pl.BlockSpec((d, tf), lambda i, j: (0, j)),
                  pl.BlockSpec((tf, d), lambda i, j: (j, 0)),
                  pl.BlockSpec((1, d), lambda i, j: (0, 0))],
        out_specs=pl.BlockSpec((tm, d), lambda i, j: (i, 0)),
        out_shape=jax.ShapeDtypeStruct((m, d), jnp.float32),
        scratch_shapes=[pltpu.VMEM((tm, d), MXU_DTYPE)],
        compiler_params=_params("parallel", "arbitrary"),
    )(h, g.reshape(1, d), w_up, w_down, g_final.reshape(1, d))


def _attn_kernel(q_ref, qall_ref, k_ref, v_ref, slope_ref, lam_ref, g_ref, o_ref,
                 s0_ref, s1_ref, m_ref, mfin0_ref, mfin1_ref, acc_ref, bias_ref, vaug_ref,
                 qq_ref, win_ref, *, tq, tk, unroll, lam_init):
    t = k_ref.shape[0]
    nk, nq = t // tk, t // tq
    rows = 2 * tq
    bi, qi = pl.program_id(1), pl.program_id(2)
    q0 = qi * tq
    nt = (((1,), (1,)), ((), ()))
    c = slope_ref[...][:, :1] * LOG2E
    s_refs, mfin_refs = (s0_ref, s1_ref), (mfin0_ref, mfin1_ref)

    strip_rows, strip_w = bias_ref.shape
    strip_origin = t + tq - strip_rows

    @pl.when((bi == 0) & (qi == 0))
    def _():
        r = lax.broadcasted_iota(jnp.int32, (strip_rows, HEAD_W), 0)
        u = lax.broadcasted_iota(jnp.int32, (strip_rows, HEAD_W), 1)
        rel = (r - u + strip_origin).astype(jnp.float32)

        def fill(a, _):
            u0 = pl.multiple_of(a * HEAD_W, HEAD_W)
            bias_ref[:, pl.ds(u0, HEAD_W)] = -c * jnp.abs(rel - u0.astype(jnp.float32))
            return 0

        lax.fori_loop(0, strip_w // HEAD_W, fill, 0)

    @pl.when(qi == 0)
    def _():
        vaug_ref[:, :HEAD_W] = v_ref[...]
        vaug_ref[:, HEAD_W:] = jnp.ones((t, HEAD_W), vaug_ref.dtype)

        def max_sq_norm(ref):
            x = ref[...].astype(jnp.float32)
            return jnp.max(jnp.sum(x * x, axis=1, keepdims=True), axis=0, keepdims=True)

        bound = jnp.sqrt(max_sq_norm(qall_ref) * max_sq_norm(k_ref)) * (
            DA ** -0.5 * LOG2E * 1.01)
        dist = jnp.minimum((2.0 * bound + ATTN_UNDERFLOW) / c, 2.0 * t)
        n_win = jnp.floor((2.0 * dist + (tq - 1)) / tk) + 2.0
        n_win = jnp.minimum(jnp.ceil(n_win / unroll) * unroll, 1.0 * nk)
        win_ref[0] = n_win[0, 0].astype(jnp.int32)
        win_ref[1] = jnp.floor(dist)[0, 0].astype(jnp.int32)

    n_win, dist = win_ref[0], win_ref[1]

    def first_tile(start):
        return jnp.clip(jnp.maximum(start - dist, 0) // tk, 0, nk - n_win)

    lo_cur = first_tile(q0)
    lo_prev = first_tile(q0 - tq)

    def sweep1(cur, r):
        k0 = pl.multiple_of((lo_cur + r) * tk, tk)
        s = lax.dot_general(qq_ref[...], k_ref[pl.ds(k0, tk), :], nt,
                            preferred_element_type=jnp.float32)
        bias = [bias_ref[:, pl.ds(pl.multiple_of(k0 - q0 - rb * strip_rows + strip_origin,
                                                 HEAD_W), tk)]
                for rb in range(tq // strip_rows)]
        st = s + jnp.concatenate(bias + bias, axis=0)
        s_refs[cur][:, r * tk:(r + 1) * tk] = st
        part = functools.reduce(
            jnp.maximum, [st[:, a * HEAD_W:(a + 1) * HEAD_W] for a in range(tk // HEAD_W)])
        m_ref[...] = jnp.maximum(m_ref[...], part)

    def sweep2(prev, r):
        k0 = pl.multiple_of((lo_prev + r) * tk, tk)
        cols = slice(r * tk, (r + 1) * tk)
        for mp in range(2):
            rs = slice(mp * tq, (mp + 1) * tq)
            m_rep = mfin_refs[prev][rs, :]
            p = jnp.exp2(s_refs[prev][rs, cols] - jnp.concatenate([m_rep] * (tk // HEAD_W), axis=1))
            acc_ref[mp] += jnp.dot(p.astype(vaug_ref.dtype), vaug_ref[pl.ds(k0, tk), :],
                                   preferred_element_type=jnp.float32)

    def start_sweep1():
        q = q_ref[...].astype(jnp.float32) * (DA ** -0.5 * LOG2E)
        lane = lax.broadcasted_iota(jnp.int32, q.shape, 1)
        qq_ref[...] = jnp.concatenate([jnp.where(lane < DA, q, 0.0),
                                       jnp.where(lane >= DA, q, 0.0)],
                                      axis=0).astype(qq_ref.dtype)
        m_ref[...] = jnp.full(m_ref.shape, -jnp.inf, jnp.float32)

    def end_sweep1(cur):
        m = jnp.max(m_ref[...], axis=1, keepdims=True)
        mfin_refs[cur][...] = jnp.broadcast_to(m, (rows, HEAD_W))

    def start_sweep2():
        acc_ref[...] = jnp.zeros(acc_ref.shape, jnp.float32)

    def end_sweep2():
        lp = lam_ref[...]
        lam = (jnp.exp(jnp.sum(lp[0:1] * lp[1:2], axis=1, keepdims=True))
               - jnp.exp(jnp.sum(lp[2:3] * lp[3:4], axis=1, keepdims=True)) + lam_init)
        o = [acc_ref[mp][:, :HEAD_W] / acc_ref[mp][:, HEAD_W:] for mp in range(2)]
        o = o[0] - lam * o[1]
        o_ref[...] = (_rms(o, g_ref[...]) * (1.0 - lam_init)).astype(o_ref.dtype)

    def step(when, cur, prev):
        for n in range(unroll, nk + 1, unroll):
            @pl.when(when & (n_win == n))
            def _():
                if cur is not None:
                    start_sweep1()
                if prev is not None:
                    start_sweep2()
                for r in range(n):
                    if cur is not None:
                        sweep1(cur, r)
                for r in range(n):
                    if prev is not None:
                        sweep2(prev, r)
                if cur is not None:
                    end_sweep1(cur)
                if prev is not None:
                    end_sweep2()

    step(qi == 0, 0, None)
    for cur in range(2):
        step((qi > 0) & (qi < nq) & (qi % 2 == cur), cur, 1 - cur)
    step(qi == nq, None, (nq - 1) % 2)


def _attention(proj, slopes, lam_p, subln_g, lam_init):
    b, t, _ = proj.shape
    tq, tk = _tile(t, ATTN_TQ), _tile(t, ATTN_TK)
    nq = t // tq
    strip_rows = min(tq, HEAD_W)
    h = N_HEADS
    return pl.pallas_call(
        functools.partial(_attn_kernel, tq=tq, tk=tk, lam_init=lam_init,
                          unroll=ATTN_UNROLL if (t // tk) % ATTN_UNROLL == 0 else 1),
        grid=(h, b, nq + 1),
        in_specs=[pl.BlockSpec((None, tq, HEAD_W),
                               lambda hi, bi, qi: (bi, jnp.minimum(qi, nq - 1), hi)),
                  pl.BlockSpec((None, t, HEAD_W), lambda hi, bi, qi: (bi, 0, hi)),
                  pl.BlockSpec((None, t, HEAD_W), lambda hi, bi, qi: (bi, 0, h + hi)),
                  pl.BlockSpec((None, t, HEAD_W), lambda hi, bi, qi: (bi, 0, 2 * h + hi)),
                  pl.BlockSpec((None, 1, HEAD_W), lambda hi, bi, qi: (hi, 0, 0)),
                  pl.BlockSpec(lam_p.shape, lambda hi, bi, qi: (0, 0)),
                  pl.BlockSpec((1, HEAD_W), lambda hi, bi, qi: (0, 0))],
        out_specs=pl.BlockSpec((None, tq, HEAD_W),
                               lambda hi, bi, qi: (bi, jnp.maximum(qi - 1, 0), hi)),
        out_shape=jax.ShapeDtypeStruct((b, t, h * HEAD_W), MXU_DTYPE),
        scratch_shapes=[pltpu.VMEM((2 * tq, t), jnp.float32),
                        pltpu.VMEM((2 * tq, t), jnp.float32),
                        pltpu.VMEM((2 * tq, HEAD_W), jnp.float32),
                        pltpu.VMEM((2 * tq, HEAD_W), jnp.float32),
                        pltpu.VMEM((2 * tq, HEAD_W), jnp.float32),
                        pltpu.VMEM((2, tq, 2 * HEAD_W), jnp.float32),
                        pltpu.VMEM((strip_rows, 2 * t + tq - strip_rows), jnp.float32),
                        pltpu.VMEM((t, 2 * HEAD_W), MXU_DTYPE),
                        pltpu.VMEM((2 * tq, HEAD_W), MXU_DTYPE),
                        pltpu.SMEM((2,), jnp.int32)],
        compiler_params=_params("parallel", "arbitrary", "arbitrary"),
    )(proj, proj, proj, proj, slopes, lam_p, subln_g.reshape(1, HEAD_W))


def _hgrn_constants(c):
    levels = []
    size = 2
    while size <= c:
        levels.append(size)
        size *= 2
    expo = np.zeros((len(levels) + 1, c, c), np.float32)
    mask = np.zeros((len(levels) + 1, c, c), np.float32)
    for li, size in enumerate(levels):
        half = size // 2
        for t in range(c):
            pos = t % size
            bd = t - pos + half - 1
            if pos >= half:
                expo[li, t, bd + 1:t + 1] = 1.0
                mask[li, t, t - pos:t - pos + half] = 1.0
            else:
                expo[li, t, t + 1:bd + 1] = 1.0
    expo[-1] = np.tril(np.ones((c, c), np.float32))
    mask[-1] = np.eye(c, dtype=np.float32)

    def pack(e):
        e = e.reshape(-1, c)
        return np.concatenate([e, e, e], axis=1)

    return (np.stack([pack(expo), pack(expo[:, ::-1, ::-1])]),
            np.stack([mask, mask[:, ::-1, ::-1]]), len(levels))


def _split3(x):
    hi = x.astype(jnp.bfloat16)
    r1 = x - hi.astype(jnp.float32)
    mid = r1.astype(jnp.bfloat16)
    lo = (r1 - mid.astype(jnp.float32)).astype(jnp.bfloat16)
    return jnp.concatenate([hi, mid, lo], axis=0)


def _hgrn_kernel(q_ref, v_ref, gate_ref, zf_ref, zb_ref, lbf_ref, lbb_ref, gn_ref,
                 expo_ref, mask_ref, o_ref, of_ref, ob_ref, sf_ref, sb_ref,
                 *, chunk, group, n_levels, out_rows):
    t = q_ref.shape[0]
    c = chunk
    n_chunks = t // c
    nt = (((1,), (1,)), ((), ()))
    tn = (((0,), (0,)), ((), ()))
    sf_ref[...] = jnp.zeros_like(sf_ref)
    sb_ref[...] = jnp.zeros_like(sb_ref)

    def gates(r0, z_ref, lb_ref):
        z = z_ref[pl.ds(r0, c), :]
        lb = lb_ref[...]
        e = jnp.exp(-jnp.abs(z))
        r = 1.0 / (1.0 + e)
        er = e * r
        pos = z >= 0.0
        f = lb + (1.0 - lb) * jnp.where(pos, r, er)
        kk = (1.0 - lb) * jnp.where(pos, er, r)
        return kk, jnp.log(f)

    def exponents(gs, d):
        both = jnp.dot(expo_ref[d], jnp.concatenate([_split3(g) for g in gs], axis=1),
                       preferred_element_type=jnp.float32)
        return [both[:, i * HEAD_W:(i + 1) * HEAD_W] for i in range(len(gs))]

    def chunk_local(r0, kk, ex_all, d):
        rows = pl.ds(r0, c)
        b = ex_all[n_levels * c:]
        ex_all = jnp.exp(ex_all)
        qf = q_ref[rows, :].astype(jnp.float32)
        v = v_ref[rows, :]
        att = mask_ref[d, n_levels] * lax.dot_general(
            qf.astype(MXU_DTYPE), kk.astype(MXU_DTYPE), nt,
            preferred_element_type=jnp.float32)
        for li in range(n_levels):
            ex = ex_all[li * c:(li + 1) * c]
            att += mask_ref[d, li] * lax.dot_general(
                (qf * ex).astype(MXU_DTYPE), (kk * ex).astype(MXU_DTYPE), nt,
                preferred_element_type=jnp.float32)
        intra = jnp.dot(att.astype(MXU_DTYPE), v, preferred_element_type=jnp.float32)
        q_dec = (qf * ex_all[n_levels * c:]).astype(MXU_DTYPE)
        b_last = b[c - 1:c] if d == 0 else b[0:1]
        kd = (kk * jnp.exp(b_last - b)).astype(MXU_DTYPE)
        update = lax.dot_general(v, kd, tn, preferred_element_type=jnp.float32)
        return rows, intra, q_dec, jnp.exp(b_last), update

    def direction(starts, z_ref, lb_ref, st_ref, out_ref, d):
        kks, gs = zip(*[gates(r0, z_ref, lb_ref) for r0 in starts])
        exs = []
        for i in range(0, len(starts), 2):
            exs += exponents(gs[i:i + 2], d)
        local = [chunk_local(r0, kk, ex, d) for r0, kk, ex in zip(starts, kks, exs)]
        st = st_ref[...]
        for rows, intra, q_dec, decay, update in local:
            out_ref[rows, :] = intra + lax.dot_general(
                q_dec, st.astype(MXU_DTYPE), nt, preferred_element_type=jnp.float32)
            st = decay * st + update
        st_ref[...] = st

    def body(n, _):
        fwd = [pl.multiple_of((n * group + i) * c, c) for i in range(group)]
        bwd = [pl.multiple_of((n_chunks - 1 - n * group - i) * c, c) for i in range(group)]
        direction(fwd, zf_ref, lbf_ref, sf_ref, of_ref, 0)
        direction(bwd, zb_ref, lbb_ref, sb_ref, ob_ref, 1)
        return 0

    lax.fori_loop(0, n_chunks // group, body, 0)

    def finish(i, _):
        rows = pl.ds(pl.multiple_of(i * out_rows, out_rows), out_rows)
        o = _rms(of_ref[rows, :] + ob_ref[rows, :], gn_ref[...])
        gate = gate_ref[rows, :].astype(jnp.float32)
        o_ref[rows, :] = (o * (gate / (1.0 + jnp.exp(-gate)))).astype(o_ref.dtype)
        return 0

    lax.fori_loop(0, t // out_rows, finish, 0)


def _hgrn(proj, zgates, lb, gn):
    b, t, _ = proj.shape
    h = N_HEADS
    c = _tile(t, HG_CHUNK)
    expo, mask, n_levels = _hgrn_constants(c)
    expo = jnp.asarray(expo, MXU_DTYPE)
    mask = jnp.asarray(mask, jnp.float32)
    out_rows = _tile(t, 512)
    group = HG_GROUP if (t // c) % HG_GROUP == 0 else 1
    head_spec = lambda off: pl.BlockSpec((None, t, HEAD_W), lambda bi, hi: (bi, 0, off + hi))
    vec_spec = pl.BlockSpec((None, 1, HEAD_W), lambda bi, hi: (hi, 0, 0))
    return pl.pallas_call(
        functools.partial(_hgrn_kernel, chunk=c, group=group, n_levels=n_levels,
                          out_rows=out_rows),
        grid=(b, h),
        in_specs=[head_spec(3 * h), head_spec(4 * h), head_spec(5 * h),
                  head_spec(0), head_spec(h), vec_spec, vec_spec, vec_spec,
                  pl.BlockSpec(expo.shape, lambda bi, hi: (0, 0, 0)),
                  pl.BlockSpec(mask.shape, lambda bi, hi: (0, 0, 0, 0))],
        out_specs=pl.BlockSpec((None, t, HEAD_W), lambda bi, hi: (bi, 0, hi)),
        out_shape=jax.ShapeDtypeStruct((b, t, h * HEAD_W), MXU_DTYPE),
        scratch_shapes=[pltpu.VMEM((t, HEAD_W), jnp.float32),
                        pltpu.VMEM((t, HEAD_W), jnp.float32),
                        pltpu.VMEM((HEAD_W, HEAD_W), jnp.float32),
                        pltpu.VMEM((HEAD_W, HEAD_W), jnp.float32)],
        compiler_params=_params("parallel", "arbitrary"),
    )(proj, proj, proj, zgates, zgates,
      lb[0].reshape(h, 1, HEAD_W), lb[1].reshape(h, 1, HEAD_W), gn.reshape(h, 1, HEAD_W),
      expo, mask)


def _conv_inproj_kernel(x_ref, g_ref, w_ref, bg_ref, u_ref, hn_ref, *, cc):
    @pl.when(pl.program_id(1) == 0)
    def _():
        hn_ref[...] = _rms(x_ref[...], g_ref[...]).astype(hn_ref.dtype)

    r = jnp.dot(hn_ref[...], w_ref[...], preferred_element_type=jnp.float32)
    bg_ref[...] = r[:, :cc].astype(bg_ref.dtype)
    u_ref[...] = (r[:, cc:2 * cc] * r[:, 2 * cc:]).astype(u_ref.dtype)


def _conv_inproj(x, g, w, tm=1024, cc=512):
    m, k = x.shape
    d = w.shape[1] // 3
    tm, cc = _tile(m, tm), _tile(d, cc)
    nb = d // cc
    w = w.reshape(k, 3, nb, cc).transpose(0, 2, 1, 3).reshape(k, 3 * d).astype(MXU_DTYPE)
    return pl.pallas_call(
        functools.partial(_conv_inproj_kernel, cc=cc),
        grid=(m // tm, nb),
        in_specs=[pl.BlockSpec((tm, k), lambda i, j: (i, 0)),
                  pl.BlockSpec((1, k), lambda i, j: (0, 0)),
                  pl.BlockSpec((k, 3 * cc), lambda i, j: (0, j))],
        out_specs=[pl.BlockSpec((tm, cc), lambda i, j: (i, j)),
                   pl.BlockSpec((tm, cc), lambda i, j: (i, j))],
        out_shape=[jax.ShapeDtypeStruct((m, d), MXU_DTYPE),
                   jax.ShapeDtypeStruct((m, d), MXU_DTYPE)],
        scratch_shapes=[pltpu.VMEM((tm, k), MXU_DTYPE)],
        compiler_params=_params("parallel", "arbitrary"),
    )(x, g.reshape(1, k), w)


def _conv_kernel(bg_ref, u_ref, up_ref, un_ref, cw_ref, w_ref, r_ref, o_ref, us_ref,
                 *, tm, cw, tiles_per_seq):
    i = pl.program_id(0)
    first = (i % tiles_per_seq) == 0
    last = (i % tiles_per_seq) == tiles_per_seq - 1
    d = u_ref.shape[1]
    ys = []
    for n in range(d // cw):
        cs = slice(n * cw, (n + 1) * cw)
        u = u_ref[:, cs].astype(jnp.float32)
        us_ref[n, 8:tm + 8, :] = u
        us_ref[n, 7:8, :] = jnp.where(
            first, 0.0, up_ref[CONV_HALO - 1:CONV_HALO, cs].astype(jnp.float32))
        us_ref[n, tm + 8:tm + 9, :] = jnp.where(
            last, 0.0, un_ref[0:1, cs].astype(jnp.float32))
        w = cw_ref[:, cs]
        conv = (w[0:1] * us_ref[n, 7:tm + 7, :] + w[1:2] * u
                + w[2:3] * us_ref[n, 9:tm + 9, :])
        ys.append((bg_ref[:, cs].astype(jnp.float32) * conv).astype(w_ref.dtype))
    o_ref[...] = r_ref[...] + jnp.dot(jnp.concatenate(ys, axis=1), w_ref[...],
                                      preferred_element_type=jnp.float32)


def _conv_mixer(bg, u, conv_w, w_out, res, seq, tm=512, cw=256):
    m, d = u.shape
    n = w_out.shape[1]
    tm, tn, cw = _tile(seq, tm), n, _tile(d, cw)
    tiles_per_seq = seq // tm
    hb = tm // CONV_HALO
    n_hb = m // CONV_HALO
    main = pl.BlockSpec((tm, d), lambda i, j: (i, 0))
    prev = pl.BlockSpec((CONV_HALO, d), lambda i, j: (jnp.maximum(i * hb - 1, 0), 0))
    nxt = pl.BlockSpec((CONV_HALO, d), lambda i, j: (jnp.minimum((i + 1) * hb, n_hb - 1), 0))
    return pl.pallas_call(
        functools.partial(_conv_kernel, tm=tm, cw=cw, tiles_per_seq=tiles_per_seq),
        grid=(m // tm, n // tn),
        in_specs=[main, main, prev, nxt,
                  pl.BlockSpec(conv_w.shape, lambda i, j: (0, 0)),
                  pl.BlockSpec((d, tn), lambda i, j: (0, j)),
                  pl.BlockSpec((tm, tn), lambda i, j: (i, j))],
        out_specs=pl.BlockSpec((tm, tn), lambda i, j: (i, j)),
        out_shape=jax.ShapeDtypeStruct((m, n), jnp.float32),
        scratch_shapes=[pltpu.VMEM((d // cw, tm + 16, cw), jnp.float32)],
        compiler_params=_params("parallel", "arbitrary"),
    )(bg, u, u, u, conv_w, w_out, res)


def kernel(x, mix_norm_g, mlp_norm_g, final_norm_g, ab_w_in, ab_w_out, diff_lambda,
           diff_subln_g, hgrn_lb, hgrn_norm_g, conv_w_in, conv_w, conv_w_out,
           mlp_w_up, mlp_w_down):
    bsz, seq, d = x.shape
    m = bsz * seq
    h = N_HEADS
    w_attn = h * HEAD_W
    n_bf = 6 * w_attn
    cast = lambda w: w.astype(MXU_DTYPE)

    hres = x.reshape(m, d)

    proj, zgates = _norm_matmul_split(hres, mix_norm_g[0], cast(ab_w_in[0]), n_bf,
                                      MXU_DTYPE, jnp.float32)
    proj = proj.reshape(bsz, seq, n_bf)
    zgates = zgates.reshape(bsz, seq, 2 * w_attn)

    lam_init = 0.8 - 0.6 * math.exp(-0.3 * 0)
    slopes = 2.0 ** (-8.0 * jnp.arange(1, h + 1, dtype=jnp.float32) / h)
    slopes = jnp.broadcast_to(slopes[:, None, None], (h, 1, HEAD_W))
    oa = _attention(proj, slopes, diff_lambda[0].astype(jnp.float32), diff_subln_g[0], lam_init)

    lb = jnp.cumsum(jax.nn.softmax(hgrn_lb.astype(jnp.float32), axis=1), axis=1)[:, 0]
    ob = _hgrn(proj, zgates, lb, hgrn_norm_g[0])

    hres = _proj2_residual(oa.reshape(m, w_attn), ob.reshape(m, w_attn),
                           cast(ab_w_out[0]), hres)
    hres = _mlp(hres, mlp_norm_g[0], cast(mlp_w_up[0]), cast(mlp_w_down[0]),
                final_norm_g, False)

    bgate, u = _conv_inproj(hres, mix_norm_g[1], conv_w_in[0])
    hres = _conv_mixer(bgate, u, conv_w[0], cast(conv_w_out[0]), hres, seq)
    hres = _mlp(hres, mlp_norm_g[1], cast(mlp_w_up[1]), cast(mlp_w_down[1]),
                final_norm_g, True)
    return hres.reshape(bsz, seq, d)
```

```python
import functools
import math

import numpy as np
import jax
import jax.numpy as jnp
from jax import lax
from jax.experimental import pallas as pl
from jax.experimental.pallas import tpu as pltpu

EPS = 1e-6
N_HEADS = 8
HEAD_W = 128
DA = 64
HG_CHUNK = 64
HG_GROUP = 8
CONV_HALO = 16
MXU_DTYPE = jnp.bfloat16
LOG2E = 1.4426950408889634
ATTN_TQ = 256
ATTN_TK = 512
ATTN_UNROLL = 4
ATTN_UNDERFLOW = 160.0
VMEM_LIMIT = 56 * 1024 * 1024


def _tile(dim, pref):
    return pref if dim % pref == 0 else dim


def _rms(x, g):
    ms = jnp.mean(x * x, axis=-1, keepdims=True)
    return x * lax.rsqrt(ms + EPS) * g


def _params(*sem):
    return pltpu.CompilerParams(dimension_semantics=sem, vmem_limit_bytes=VMEM_LIMIT)


def _norm_matmul_split_kernel(x_ref, g_ref, w_ref, lo_ref, hi_ref, hn_ref, *, n_lo):
    j = pl.program_id(1)

    def product():
        return jnp.dot(hn_ref[...], w_ref[...], preferred_element_type=jnp.float32)

    @pl.when(j == 0)
    def _():
        hn_ref[...] = _rms(x_ref[...], g_ref[...]).astype(hn_ref.dtype)
        lo_ref[...] = product().astype(lo_ref.dtype)

    @pl.when((j > 0) & (j < n_lo))
    def _():
        lo_ref[...] = product().astype(lo_ref.dtype)

    @pl.when(j >= n_lo)
    def _():
        hi_ref[...] = product().astype(hi_ref.dtype)


def _norm_matmul_split(x, g, w, n_split, lo_dtype, hi_dtype, tm=1024, tn=1024):
    m, k = x.shape
    n = w.shape[1]
    tm, tn = _tile(m, tm), _tile(n_split, tn)
    assert n_split % tn == 0 and (n - n_split) % tn == 0
    n_lo = n_split // tn
    return pl.pallas_call(
        functools.partial(_norm_matmul_split_kernel, n_lo=n_lo),
        grid=(m // tm, n // tn),
        in_specs=[pl.BlockSpec((tm, k), lambda i, j: (i, 0)),
                  pl.BlockSpec((1, k), lambda i, j: (0, 0)),
                  pl.BlockSpec((k, tn), lambda i, j: (0, j))],
        out_specs=[pl.BlockSpec((tm, tn), lambda i, j: (i, jnp.minimum(j, n_lo - 1))),
                   pl.BlockSpec((tm, tn), lambda i, j: (i, jnp.maximum(j - n_lo, 0)))],
        out_shape=[jax.ShapeDtypeStruct((m, n_split), lo_dtype),
                   jax.ShapeDtypeStruct((m, n - n_split), hi_dtype)],
        scratch_shapes=[pltpu.VMEM((tm, k), MXU_DTYPE)],
        compiler_params=_params("parallel", "arbitrary"),
    )(x, g.reshape(1, k), w)


def _proj2_kernel(a_ref, b_ref, wa_ref, wb_ref, r_ref, o_ref):
    acc = jnp.dot(a_ref[...], wa_ref[...], preferred_element_type=jnp.float32)
    acc += jnp.dot(b_ref[...], wb_ref[...], preferred_element_type=jnp.float32)
    o_ref[...] = r_ref[...] + acc


def _proj2_residual(a, b, w, res, tm=512, tn=2048):
    m, ka = a.shape
    kb = b.shape[1]
    n = w.shape[1]
    assert ka == kb and w.shape[0] == ka + kb
    tm, tn = _tile(m, tm), _tile(n, tn)
    return pl.pallas_call(
        _proj2_kernel,
        grid=(m // tm, n // tn),
        in_specs=[pl.BlockSpec((tm, ka), lambda i, j: (i, 0)),
                  pl.BlockSpec((tm, kb), lambda i, j: (i, 0)),
                  pl.BlockSpec((ka, tn), lambda i, j: (0, j)),
                  pl.BlockSpec((kb, tn), lambda i, j: (1, j)),
                  pl.BlockSpec((tm, tn), lambda i, j: (i, j))],
        out_specs=pl.BlockSpec((tm, tn), lambda i, j: (i, j)),
        out_shape=jax.ShapeDtypeStruct((m, n), jnp.float32),
        compiler_params=_params("parallel", "arbitrary"),
    )(a, b, w, w, res)


def _mlp_kernel(h_ref, g_ref, wu_ref, wd_ref, gf_ref, o_ref, hn_ref, *, final_norm):
    j = pl.program_id(1)

    def chunk():
        a = jnp.dot(hn_ref[...], wu_ref[...], preferred_element_type=jnp.float32)
        a = jnp.maximum(a, 0.0)
        a = (a * a).astype(wd_ref.dtype)
        return jnp.dot(a, wd_ref[...], preferred_element_type=jnp.float32)

    @pl.when(j == 0)
    def _():
        hn_ref[...] = _rms(h_ref[...], g_ref[...]).astype(hn_ref.dtype)
        o_ref[...] = h_ref[...] + chunk()

    @pl.when(j > 0)
    def _():
        o_ref[...] += chunk()

    if final_norm:
        @pl.when(j == pl.num_programs(1) - 1)
        def _():
            o_ref[...] = _rms(o_ref[...], gf_ref[...])


def _mlp(h, g, w_up, w_down, g_final, final_norm, tm=1024, tf=512):
    m, d = h.shape
    f = w_up.shape[1]
    tm, tf = _tile(m, tm), _tile(f, tf)
    return pl.pallas_call(
        functools.partial(_mlp_kernel, final_norm=final_norm),
        grid=(m // tm, f // tf),
        in_specs=[pl.BlockSpec((tm, d), lambda i, j: (i, 0)),
                  pl.BlockSpec((1, d), lambda i, j: (0, 0)),
                  pl.BlockSpec((d, tf), lambda i, j: (0, j)),
                  pl.BlockSpec((tf, d), lambda i, j: (j, 0)),
                  pl.BlockSpec((1, d), lambda i, j: (0, 0))],
        out_specs=pl.BlockSpec((tm, d), lambda i, j: (i, 0)),
        out_shape=jax.ShapeDtypeStruct((m, d), jnp.float32),
        scratch_shapes=[pltpu.VMEM((tm, d), MXU_DTYPE)],
        compiler_params=_params("parallel", "arbitrary"),
    )(h, g.reshape(1, d), w_up, w_down, g_final.reshape(1, d))


def _attn_kernel(q_ref, qall_ref, k_ref, v_ref, slope_ref, lam_ref, g_ref, o_ref,
                 s0_ref, s1_ref, m_ref, mfin0_ref, mfin1_ref, acc_ref, bias_ref, vaug_ref,
                 qq_ref, win_ref, *, tq, tk, unroll, lam_init):
    t = k_ref.shape[0]
    nk, nq = t // tk, t // tq
    rows = 2 * tq
    bi, qi = pl.program_id(1), pl.program_id(2)
    q0 = qi * tq
    nt = (((1,), (1,)), ((), ()))
    c = slope_ref[...][:, :1] * LOG2E
    s_refs, mfin_refs = (s0_ref, s1_ref), (mfin0_ref, mfin1_ref)

    strip_rows, strip_w = bias_ref.shape
    strip_origin = t + tq - strip_rows

    @pl.when((bi == 0) & (qi == 0))
    def _():
        r = lax.broadcasted_iota(jnp.int32, (strip_rows, HEAD_W), 0)
        u = lax.broadcasted_iota(jnp.int32, (strip_rows, HEAD_W), 1)
        rel = (r - u + strip_origin).astype(jnp.float32)

        def fill(a, _):
            u0 = pl.multiple_of(a * HEAD_W, HEAD_W)
            bias_ref[:, pl.ds(u0, HEAD_W)] = -c * jnp.abs(rel - u0.astype(jnp.float32))
            return 0

        lax.fori_loop(0, strip_w // HEAD_W, fill, 0)

    @pl.when(qi == 0)
    def _():
        vaug_ref[:, :HEAD_W] = v_ref[...]
        vaug_ref[:, HEAD_W:] = jnp.ones((t, HEAD_W), vaug_ref.dtype)

        def max_sq_norm(ref):
            x = ref[...].astype(jnp.float32)
            return jnp.max(jnp.sum(x * x, axis=1, keepdims=True), axis=0, keepdims=True)

        bound = jnp.sqrt(max_sq_norm(qall_ref) * max_sq_norm(k_ref)) * (
            DA ** -0.5 * LOG2E * 1.01)
        dist = jnp.minimum((2.0 * bound + ATTN_UNDERFLOW) / c, 2.0 * t)
        n_win = jnp.floor((2.0 * dist + (tq - 1)) / tk) + 2.0
        n_win = jnp.minimum(jnp.ceil(n_win / unroll) * unroll, 1.0 * nk)
        win_ref[0] = n_win[0, 0].astype(jnp.int32)
        win_ref[1] = jnp.floor(dist)[0, 0].astype(jnp.int32)

    n_win, dist = win_ref[0], win_ref[1]

    def first_tile(start):
        return jnp.clip(jnp.maximum(start - dist, 0) // tk, 0, nk - n_win)

    lo_cur = first_tile(q0)
    lo_prev = first_tile(q0 - tq)

    def sweep1(cur, r):
        k0 = pl.multiple_of((lo_cur + r) * tk, tk)
        s = lax.dot_general(qq_ref[...], k_ref[pl.ds(k0, tk), :], nt,
                            preferred_element_type=jnp.float32)
        bias = [bias_ref[:, pl.ds(pl.multiple_of(k0 - q0 - rb * strip_rows + strip_origin,
                                                 HEAD_W), tk)]
                for rb in range(tq // strip_rows)]
        st = s + jnp.concatenate(bias + bias, axis=0)
        s_refs[cur][:, r * tk:(r + 1) * tk] = st
        part = functools.reduce(
            jnp.maximum, [st[:, a * HEAD_W:(a + 1) * HEAD_W] for a in range(tk // HEAD_W)])
        m_ref[...] = jnp.maximum(m_ref[...], part)

    def sweep2(prev, r):
        k0 = pl.multiple_of((lo_prev + r) * tk, tk)
        cols = slice(r * tk, (r + 1) * tk)
        for mp in range(2):
            rs = slice(mp * tq, (mp + 1) * tq)
            m_rep = mfin_refs[prev][rs, :]
            p = jnp.exp2(s_refs[prev][rs, cols] - jnp.concatenate([m_rep] * (tk // HEAD_W), axis=1))
            acc_ref[mp] += jnp.dot(p.astype(vaug_ref.dtype), vaug_ref[pl.ds(k0, tk), :],
                                   preferred_element_type=jnp.float32)

    def start_sweep1():
        q = q_ref[...].astype(jnp.float32) * (DA ** -0.5 * LOG2E)
        lane = lax.broadcasted_iota(jnp.int32, q.shape, 1)
        qq_ref[...] = jnp.concatenate([jnp.where(lane < DA, q, 0.0),
                                       jnp.where(lane >= DA, q, 0.0)],
                                      axis=0).astype(qq_ref.dtype)
        m_ref[...] = jnp.full(m_ref.shape, -jnp.inf, jnp.float32)

    def end_sweep1(cur):
        m = jnp.max(m_ref[...], axis=1, keepdims=True)
        mfin_refs[cur][...] = jnp.broadcast_to(m, (rows, HEAD_W))

    def start_sweep2():
        acc_ref[...] = jnp.zeros(acc_ref.shape, jnp.float32)

    def end_sweep2():
        lp = lam_ref[...]
        lam = (jnp.exp(jnp.sum(lp[0:1] * lp[1:2], axis=1, keepdims=True))
               - jnp.exp(jnp.sum(lp[2:3] * lp[3:4], axis=1, keepdims=True)) + lam_init)
        o = [acc_ref[mp][:, :HEAD_W] / acc_ref[mp][:, HEAD_W:] for mp in range(2)]
        o = o[0] - lam * o[1]
        o_ref[...] = (_rms(o, g_ref[...]) * (1.0 - lam_init)).astype(o_ref.dtype)

    def step(when, cur, prev):
        for n in range(unroll, nk + 1, unroll):
            @pl.when(when & (n_win == n))
            def _():
                if cur is not None:
                    start_sweep1()
                if prev is not None:
                    start_sweep2()
                for r in range(n):
                    if cur is not None:
                        sweep1(cur, r)
                for r in range(n):
                    if prev is not None:
                        sweep2(prev, r)
                if cur is not None:
                    end_sweep1(cur)
                if prev is not None:
                    end_sweep2()

    step(qi == 0, 0, None)
    for cur in range(2):
        step((qi > 0) & (qi < nq) & (qi % 2 == cur), cur, 1 - cur)
    step(qi == nq, None, (nq - 1) % 2)


def _attention(proj, slopes, lam_p, subln_g, lam_init):
    b, t, _ = proj.shape
    tq, tk = _tile(t, ATTN_TQ), _tile(t, ATTN_TK)
    nq = t // tq
    strip_rows = min(tq, HEAD_W)
    h = N_HEADS
    return pl.pallas_call(
        functools.partial(_attn_kernel, tq=tq, tk=tk, lam_init=lam_init,
                          unroll=ATTN_UNROLL if (t // tk) % ATTN_UNROLL == 0 else 1),
        grid=(h, b, nq + 1),
        in_specs=[pl.BlockSpec((None, tq, HEAD_W),
                               lambda hi, bi, qi: (bi, jnp.minimum(qi, nq - 1), hi)),
                  pl.BlockSpec((None, t, HEAD_W), lambda hi, bi, qi: (bi, 0, hi)),
                  pl.BlockSpec((None, t, HEAD_W), lambda hi, bi, qi: (bi, 0, h + hi)),
                  pl.BlockSpec((None, t, HEAD_W), lambda hi, bi, qi: (bi, 0, 2 * h + hi)),
                  pl.BlockSpec((None, 1, HEAD_W), lambda hi, bi, qi: (hi, 0, 0)),
                  pl.BlockSpec(lam_p.shape, lambda hi, bi, qi: (0, 0)),
                  pl.BlockSpec((1, HEAD_W), lambda hi, bi, qi: (0, 0))],
        out_specs=pl.BlockSpec((None, tq, HEAD_W),
                               lambda hi, bi, qi: (bi, jnp.maximum(qi - 1, 0), hi)),
        out_shape=jax.ShapeDtypeStruct((b, t, h * HEAD_W), MXU_DTYPE),
        scratch_shapes=[pltpu.VMEM((2 * tq, t), jnp.float32),
                        pltpu.VMEM((2 * tq, t), jnp.float32),
                        pltpu.VMEM((2 * tq, HEAD_W), jnp.float32),
                        pltpu.VMEM((2 * tq, HEAD_W), jnp.float32),
                        pltpu.VMEM((2 * tq, HEAD_W), jnp.float32),
                        pltpu.VMEM((2, tq, 2 * HEAD_W), jnp.float32),
                        pltpu.VMEM((strip_rows, 2 * t + tq - strip_rows), jnp.float32),
                        pltpu.VMEM((t, 2 * HEAD_W), MXU_DTYPE),
                        pltpu.VMEM((2 * tq, HEAD_W), MXU_DTYPE),
                        pltpu.SMEM((2,), jnp.int32)],
        compiler_params=_params("parallel", "arbitrary", "arbitrary"),
    )(proj, proj, proj, proj, slopes, lam_p, subln_g.reshape(1, HEAD_W))


def _hgrn_constants(c):
    levels = []
    size = 2
    while size <= c:
        levels.append(size)
        size *= 2
    expo = np.zeros((len(levels) + 1, c, c), np.float32)
    mask = np.zeros((len(levels) + 1, c, c), np.float32)
    for li, size in enumerate(levels):
        half = size // 2
        for t in range(c):
            pos = t % size
            bd = t - pos + half - 1
            if pos >= half:
                expo[li, t, bd + 1:t + 1] = 1.0
                mask[li, t, t - pos:t - pos + half] = 1.0
            else:
                expo[li, t, t + 1:bd + 1] = 1.0
    expo[-1] = np.tril(np.ones((c, c), np.float32))
    mask[-1] = np.eye(c, dtype=np.float32)

    def pack(e):
        e = e[1:].reshape(-1, c)
        return np.concatenate([e, e, e], axis=1)

    return (np.stack([pack(expo), pack(expo[:, ::-1, ::-1])]),
            np.stack([mask, mask[:, ::-1, ::-1]]), len(levels))


def _split3(x):
    hi = x.astype(jnp.bfloat16)
    r1 = x - hi.astype(jnp.float32)
    mid = r1.astype(jnp.bfloat16)
    lo = (r1 - mid.astype(jnp.float32)).astype(jnp.bfloat16)
    return jnp.concatenate([hi, mid, lo], axis=0)


def _hgrn_kernel(q_ref, v_ref, gate_ref, zf_ref, zb_ref, lbf_ref, lbb_ref, gn_ref,
                 expo_ref, mask_ref, o_ref, of_ref, ob_ref, sf_ref, sb_ref,
                 *, chunk, group, n_levels, out_rows):
    t = q_ref.shape[0]
    c = chunk
    n_chunks = t // c
    nt = (((1,), (1,)), ((), ()))
    tn = (((0,), (0,)), ((), ()))
    sf_ref[...] = jnp.zeros_like(sf_ref)
    sb_ref[...] = jnp.zeros_like(sb_ref)

    def gates(r0, z_ref, lb_ref):
        z = z_ref[pl.ds(r0, c), :]
        lb = lb_ref[...]
        e = jnp.exp(-jnp.abs(z))
        r = 1.0 / (1.0 + e)
        er = e * r
        pos = z >= 0.0
        f = lb + (1.0 - lb) * jnp.where(pos, r, er)
        kk = (1.0 - lb) * jnp.where(pos, er, r)
        return kk, f, jnp.log(f)

    def exponents(gs, d):
        both = jnp.dot(expo_ref[d], jnp.concatenate([_split3(g) for g in gs], axis=1),
                       preferred_element_type=jnp.float32)
        return [both[:, i * HEAD_W:(i + 1) * HEAD_W] for i in range(len(gs))]

    def chunk_local(r0, kk, f, ex_all, d):
        rows = pl.ds(r0, c)
        n_mm = n_levels - 1
        b = ex_all[n_mm * c:]
        ex_all = jnp.exp(ex_all)
        odd = lax.broadcasted_iota(jnp.int32, f.shape, 0) % 2
        ex_pair = jnp.where(odd == 1 - d, f, 1.0)
        qf = q_ref[rows, :].astype(jnp.float32)
        v = v_ref[rows, :]
        att = mask_ref[d, n_levels] * lax.dot_general(
            qf.astype(MXU_DTYPE), kk.astype(MXU_DTYPE), nt,
            preferred_element_type=jnp.float32)
        for li in range(n_levels):
            ex = ex_pair if li == 0 else ex_all[(li - 1) * c:li * c]
            att += mask_ref[d, li] * lax.dot_general(
                (qf * ex).astype(MXU_DTYPE), (kk * ex).astype(MXU_DTYPE), nt,
                preferred_element_type=jnp.float32)
        intra = jnp.dot(att.astype(MXU_DTYPE), v, preferred_element_type=jnp.float32)
        q_dec = (qf * ex_all[n_mm * c:]).astype(MXU_DTYPE)
        b_last = b[c - 1:c] if d == 0 else b[0:1]
        kd = (kk * jnp.exp(b_last - b)).astype(MXU_DTYPE)
        update = lax.dot_general(v, kd, tn, preferred_element_type=jnp.float32)
        return rows, intra, q_dec, jnp.exp(b_last), update

    def direction(starts, z_ref, lb_ref, st_ref, out_ref, d):
        kks, fs, gs = zip(*[gates(r0, z_ref, lb_ref) for r0 in starts])
        exs = []
        for i in range(0, len(starts), 2):
            exs += exponents(gs[i:i + 2], d)
        local = [chunk_local(r0, kk, f, ex, d)
                 for r0, kk, f, ex in zip(starts, kks, fs, exs)]
        st = st_ref[...]
        for rows, intra, q_dec, decay, update in local:
            out_ref[rows, :] = intra + lax.dot_general(
                q_dec, st.astype(MXU_DTYPE), nt, preferred_element_type=jnp.float32)
            st = decay * st + update
        st_ref[...] = st

    def body(n, _):
        fwd = [pl.multiple_of((n * group + i) * c, c) for i in range(group)]
        bwd = [pl.multiple_of((n_chunks - 1 - n * group - i) * c, c) for i in range(group)]
        direction(fwd, zf_ref, lbf_ref, sf_ref, of_ref, 0)
        direction(bwd, zb_ref, lbb_ref, sb_ref, ob_ref, 1)
        return 0

    lax.fori_loop(0, n_chunks // group, body, 0)

    def finish(i, _):
        rows = pl.ds(pl.multiple_of(i * out_rows, out_rows), out_rows)
        o = _rms(of_ref[rows, :] + ob_ref[rows, :], gn_ref[...])
        gate = gate_ref[rows, :].astype(jnp.float32)
        o_ref[rows, :] = (o * (gate / (1.0 + jnp.exp(-gate)))).astype(o_ref.dtype)
        return 0

    lax.fori_loop(0, t // out_rows, finish, 0)


def _hgrn(proj, zgates, lb, gn):
    b, t, _ = proj.shape
    h = N_HEADS
    c = _tile(t, HG_CHUNK)
    expo, mask, n_levels = _hgrn_constants(c)
    expo = jnp.asarray(expo, MXU_DTYPE)
    mask = jnp.asarray(mask, jnp.float32)
    out_rows = _tile(t, 512)
    group = HG_GROUP if (t // c) % HG_GROUP == 0 else 1
    head_spec = lambda off: pl.BlockSpec((None, t, HEAD_W), lambda bi, hi: (bi, 0, off + hi))
    vec_spec = pl.BlockSpec((None, 1, HEAD_W), lambda bi, hi: (hi, 0, 0))
    return pl.pallas_call(
        functools.partial(_hgrn_kernel, chunk=c, group=group, n_levels=n_levels,
                          out_rows=out_rows),
        grid=(b, h),
        in_specs=[head_spec(3 * h), head_spec(4 * h), head_spec(5 * h),
                  head_spec(0), head_spec(h), vec_spec, vec_spec, vec_spec,
                  pl.BlockSpec(expo.shape, lambda bi, hi: (0, 0, 0)),
                  pl.BlockSpec(mask.shape, lambda bi, hi: (0, 0, 0, 0))],
        out_specs=pl.BlockSpec((None, t, HEAD_W), lambda bi, hi: (bi, 0, hi)),
        out_shape=jax.ShapeDtypeStruct((b, t, h * HEAD_W), MXU_DTYPE),
        scratch_shapes=[pltpu.VMEM((t, HEAD_W), jnp.float32),
                        pltpu.VMEM((t, HEAD_W), jnp.float32),
                        pltpu.VMEM((HEAD_W, HEAD_W), jnp.float32),
                        pltpu.VMEM((HEAD_W, HEAD_W), jnp.float32)],
        compiler_params=_params("parallel", "arbitrary"),
    )(proj, proj, proj, zgates, zgates,
      lb[0].reshape(h, 1, HEAD_W), lb[1].reshape(h, 1, HEAD_W), gn.reshape(h, 1, HEAD_W),
      expo, mask)


def _conv_inproj_kernel(x_ref, g_ref, w_ref, bg_ref, u_ref, hn_ref, *, cc):
    def block():
        r = jnp.dot(hn_ref[...], w_ref[...], preferred_element_type=jnp.float32)
        bg_ref[...] = r[:, :cc].astype(bg_ref.dtype)
        u_ref[...] = (r[:, cc:2 * cc] * r[:, 2 * cc:]).astype(u_ref.dtype)

    @pl.when(pl.program_id(1) == 0)
    def _():
        hn_ref[...] = _rms(x_ref[...], g_ref[...]).astype(hn_ref.dtype)
        block()

    @pl.when(pl.program_id(1) > 0)
    def _():
        block()


def _conv_inproj(x, g, w, tm=1024, cc=512):
    m, k = x.shape
    d = w.shape[1] // 3
    tm, cc = _tile(m, tm), _tile(d, cc)
    nb = d // cc
    w = w.reshape(k, 3, nb, cc).transpose(0, 2, 1, 3).reshape(k, 3 * d).astype(MXU_DTYPE)
    return pl.pallas_call(
        functools.partial(_conv_inproj_kernel, cc=cc),
        grid=(m // tm, nb),
        in_specs=[pl.BlockSpec((tm, k), lambda i, j: (i, 0)),
                  pl.BlockSpec((1, k), lambda i, j: (0, 0)),
                  pl.BlockSpec((k, 3 * cc), lambda i, j: (0, j))],
        out_specs=[pl.BlockSpec((tm, cc), lambda i, j: (i, j)),
                   pl.BlockSpec((tm, cc), lambda i, j: (i, j))],
        out_shape=[jax.ShapeDtypeStruct((m, d), MXU_DTYPE),
                   jax.ShapeDtypeStruct((m, d), MXU_DTYPE)],
        scratch_shapes=[pltpu.VMEM((tm, k), MXU_DTYPE)],
        compiler_params=_params("parallel", "arbitrary"),
    )(x, g.reshape(1, k), w)


def _conv_kernel(bg_ref, u_ref, up_ref, un_ref, cw_ref, w_ref, r_ref, o_ref, us_ref,
                 *, tm, cw, tiles_per_seq):
    i = pl.program_id(0)
    first = (i % tiles_per_seq) == 0
    last = (i % tiles_per_seq) == tiles_per_seq - 1
    d = u_ref.shape[1]
    ys = []
    for n in range(d // cw):
        cs = slice(n * cw, (n + 1) * cw)
        u = u_ref[:, cs].astype(jnp.float32)
        us_ref[n, 8:tm + 8, :] = u
        us_ref[n, 7:8, :] = jnp.where(
            first, 0.0, up_ref[CONV_HALO - 1:CONV_HALO, cs].astype(jnp.float32))
        us_ref[n, tm + 8:tm + 9, :] = jnp.where(
            last, 0.0, un_ref[0:1, cs].astype(jnp.float32))
        w = cw_ref[:, cs]
        conv = (w[0:1] * us_ref[n, 7:tm + 7, :] + w[1:2] * u
                + w[2:3] * us_ref[n, 9:tm + 9, :])
        ys.append((bg_ref[:, cs].astype(jnp.float32) * conv).astype(w_ref.dtype))
    o_ref[...] = r_ref[...] + jnp.dot(jnp.concatenate(ys, axis=1), w_ref[...],
                                      preferred_element_type=jnp.float32)


def _conv_mixer(bg, u, conv_w, w_out, res, seq, tm=512, cw=256):
    m, d = u.shape
    n = w_out.shape[1]
    tm, tn, cw = _tile(seq, tm), n, _tile(d, cw)
    tiles_per_seq = seq // tm
    hb = tm // CONV_HALO
    n_hb = m // CONV_HALO
    main = pl.BlockSpec((tm, d), lambda i, j: (i, 0))
    prev = pl.BlockSpec((CONV_HALO, d), lambda i, j: (jnp.maximum(i * hb - 1, 0), 0))
    nxt = pl.BlockSpec((CONV_HALO, d), lambda i, j: (jnp.minimum((i + 1) * hb, n_hb - 1), 0))
    return pl.pallas_call(
        functools.partial(_conv_kernel, tm=tm, cw=cw, tiles_per_seq=tiles_per_seq),
        grid=(m // tm, n // tn),
        in_specs=[main, main, prev, nxt,
                  pl.BlockSpec(conv_w.shape, lambda i, j: (0, 0)),
                  pl.BlockSpec((d, tn), lambda i, j: (0, j)),
                  pl.BlockSpec((tm, tn), lambda i, j: (i, j))],
        out_specs=pl.BlockSpec((tm, tn), lambda i, j: (i, j)),
        out_shape=jax.ShapeDtypeStruct((m, n), jnp.float32),
        scratch_shapes=[pltpu.VMEM((d // cw, tm + 16, cw), jnp.float32)],
        compiler_params=_params("parallel", "arbitrary"),
    )(bg, u, u, u, conv_w, w_out, res)


def kernel(x, mix_norm_g, mlp_norm_g, final_norm_g, ab_w_in, ab_w_out, diff_lambda,
           diff_subln_g, hgrn_lb, hgrn_norm_g, conv_w_in, conv_w, conv_w_out,
           mlp_w_up, mlp_w_down):
    bsz, seq, d = x.shape
    m = bsz * seq
    h = N_HEADS
    w_attn = h * HEAD_W
    n_bf = 6 * w_attn
    cast = lambda w: w.astype(MXU_DTYPE)

    hres = x.reshape(m, d)

    proj, zgates = _norm_matmul_split(hres, mix_norm_g[0], cast(ab_w_in[0]), n_bf,
                                      MXU_DTYPE, jnp.float32)
    proj = proj.reshape(bsz, seq, n_bf)
    zgates = zgates.reshape(bsz, seq, 2 * w_attn)

    lam_init = 0.8 - 0.6 * math.exp(-0.3 * 0)
    slopes = 2.0 ** (-8.0 * jnp.arange(1, h + 1, dtype=jnp.float32) / h)
    slopes = jnp.broadcast_to(slopes[:, None, None], (h, 1, HEAD_W))
    oa = _attention(proj, slopes, diff_lambda[0].astype(jnp.float32), diff_subln_g[0], lam_init)

    lb = jnp.cumsum(jax.nn.softmax(hgrn_lb.astype(jnp.float32), axis=1), axis=1)[:, 0]
    ob = _hgrn(proj, zgates, lb, hgrn_norm_g[0])

    hres = _proj2_residual(oa.reshape(m, w_attn), ob.reshape(m, w_attn),
                           cast(ab_w_out[0]), hres)
    hres = _mlp(hres, mlp_norm_g[0], cast(mlp_w_up[0]), cast(mlp_w_down[0]),
                final_norm_g, False)

    bgate, u = _conv_inproj(hres, mix_norm_g[1], conv_w_in[0])
    hres = _conv_mixer(bgate, u, conv_w[0], cast(conv_w_out[0]), hres, seq)
    hres = _mlp(hres, mlp_norm_g[1], cast(mlp_w_up[1]), cast(mlp_w_down[1]),
                final_norm_g, True)
    return hres.reshape(bsz, seq, d)
```

```python
import functools
import math

import numpy as np
import jax
import jax.numpy as jnp
from jax import lax
from jax.experimental import pallas as pl
from jax.experimental.pallas import tpu as pltpu

EPS = 1e-6
N_HEADS = 8
HEAD_W = 128
DA = 64
HG_CHUNK = 64
HG_GROUP = 8
CONV_HALO = 16
MXU_DTYPE = jnp.bfloat16
LOG2E = 1.4426950408889634
ATTN_TQ = 256
ATTN_TK = 512
ATTN_UNROLL = 4
ATTN_UNDERFLOW = 160.0
VMEM_LIMIT = 56 * 1024 * 1024


def _tile(dim, pref):
    return pref if dim % pref == 0 else dim


def _rms(x, g):
    ms = jnp.mean(x * x, axis=-1, keepdims=True)
    return x * lax.rsqrt(ms + EPS) * g


def _params(*sem):
    return pltpu.CompilerParams(dimension_semantics=sem, vmem_limit_bytes=VMEM_LIMIT)


def _norm_matmul_split_kernel(x_ref, g_ref, w_ref, lo_ref, hi_ref, hn_ref, *, n_lo):
    j = pl.program_id(1)

    def product():
        return jnp.dot(hn_ref[...], w_ref[...], preferred_element_type=jnp.float32)

    @pl.when(j == 0)
    def _():
        hn_ref[...] = _rms(x_ref[...], g_ref[...]).astype(hn_ref.dtype)
        lo_ref[...] = product().astype(lo_ref.dtype)

    @pl.when((j > 0) & (j < n_lo))
    def _():
        lo_ref[...] = product().astype(lo_ref.dtype)

    @pl.when(j >= n_lo)
    def _():
        hi_ref[...] = product().astype(hi_ref.dtype)


def _norm_matmul_split(x, g, w, n_split, lo_dtype, hi_dtype, tm=1024, tn=1024):
    m, k = x.shape
    n = w.shape[1]
    tm, tn = _tile(m, tm), _tile(n_split, tn)
    assert n_split % tn == 0 and (n - n_split) % tn == 0
    n_lo = n_split // tn
    return pl.pallas_call(
        functools.partial(_norm_matmul_split_kernel, n_lo=n_lo),
        grid=(m // tm, n // tn),
        in_specs=[pl.BlockSpec((tm, k), lambda i, j: (i, 0)),
                  pl.BlockSpec((1, k), lambda i, j: (0, 0)),
                  pl.BlockSpec((k, tn), lambda i, j: (0, j))],
        out_specs=[pl.BlockSpec((tm, tn), lambda i, j: (i, jnp.minimum(j, n_lo - 1))),
                   pl.BlockSpec((tm, tn), lambda i, j: (i, jnp.maximum(j - n_lo, 0)))],
        out_shape=[jax.ShapeDtypeStruct((m, n_split), lo_dtype),
                   jax.ShapeDtypeStruct((m, n - n_split), hi_dtype)],
        scratch_shapes=[pltpu.VMEM((tm, k), MXU_DTYPE)],
        compiler_params=_params("parallel", "arbitrary"),
    )(x, g.reshape(1, k), w)


def _proj2_kernel(a_ref, b_ref, wa_ref, wb_ref, r_ref, o_ref):
    acc = jnp.dot(a_ref[...], wa_ref[...], preferred_element_type=jnp.float32)
    acc += jnp.dot(b_ref[...], wb_ref[...], preferred_element_type=jnp.float32)
    o_ref[...] = r_ref[...] + acc


def _proj2_residual(a, b, w, res, tm=512, tn=2048):
    m, ka = a.shape
    kb = b.shape[1]
    n = w.shape[1]
    assert ka == kb and w.shape[0] == ka + kb
    tm, tn = _tile(m, tm), _tile(n, tn)
    return pl.pallas_call(
        _proj2_kernel,
        grid=(m // tm, n // tn),
        in_specs=[pl.BlockSpec((tm, ka), lambda i, j: (i, 0)),
                  pl.BlockSpec((tm, kb), lambda i, j: (i, 0)),
                  pl.BlockSpec((ka, tn), lambda i, j: (0, j)),
                  pl.BlockSpec((kb, tn), lambda i, j: (1, j)),
                  pl.BlockSpec((tm, tn), lambda i, j: (i, j))],
        out_specs=pl.BlockSpec((tm, tn), lambda i, j: (i, j)),
        out_shape=jax.ShapeDtypeStruct((m, n), jnp.float32),
        compiler_params=_params("parallel", "arbitrary"),
    )(a, b, w, w, res)


def _mlp_kernel(h_ref, g_ref, wu_ref, wd_ref, gf_ref, o_ref, hn_ref, *, final_norm):
    j = pl.program_id(1)

    def chunk():
        a = jnp.dot(hn_ref[...], wu_ref[...], preferred_element_type=jnp.float32)
        a = jnp.maximum(a, 0.0)
        a = (a * a).astype(wd_ref.dtype)
        return jnp.dot(a, wd_ref[...], preferred_element_type=jnp.float32)

    @pl.when(j == 0)
    def _():
        hn_ref[...] = _rms(h_ref[...], g_ref[...]).astype(hn_ref.dtype)
        o_ref[...] = h_ref[...] + chunk()

    @pl.when(j > 0)
    def _():
        o_ref[...] += chunk()

    if final_norm:
        @pl.when(j == pl.num_programs(1) - 1)
        def _():
            o_ref[...] = _rms(o_ref[...], gf_ref[...])


def _mlp(h, g, w_up, w_down, g_final, final_norm, tm=1024, tf=512):
    m, d = h.shape
    f = w_up.shape[1]
    tm, tf = _tile(m, tm), _tile(f, tf)
    return pl.pallas_call(
        functools.partial(_mlp_kernel, final_norm=final_norm),
        grid=(m // tm, f // tf),
        in_specs=[pl.BlockSpec((tm, d), lambda i, j: (i, 0)),
                  pl.BlockSpec((1, d), lambda i, j: (0, 0)),
                  pl.BlockSpec((d, tf), lambda i, j: (0, j)),
                  pl.BlockSpec((tf, d), lambda i, j: (j, 0)),
                  pl.BlockSpec((1, d), lambda i, j: (0, 0))],
        out_specs=pl.BlockSpec((tm, d), lambda i, j: (i, 0)),
        out_shape=jax.ShapeDtypeStruct((m, d), jnp.float32),
        scratch_shapes=[pltpu.VMEM((tm, d), MXU_DTYPE)],
        compiler_params=_params("parallel", "arbitrary"),
    )(h, g.reshape(1, d), w_up, w_down, g_final.reshape(1, d))


def _attn_kernel(q_ref, qall_ref, k_ref, v_ref, slope_ref, lam_ref, g_ref, o_ref,
                 s0_ref, s1_ref, m_ref, mfin0_ref, mfin1_ref, acc_ref, bias_ref, vaug_ref,
                 qq_ref, win_ref, *, tq, tk, unroll, lam_init):
    t = k_ref.shape[0]
    nk, nq = t // tk, t // tq
    rows = 2 * tq
    bi, qi = pl.program_id(1), pl.program_id(2)
    q0 = qi * tq
    nt = (((1,), (1,)), ((), ()))
    c = slope_ref[...][:, :1] * LOG2E
    s_refs, mfin_refs = (s0_ref, s1_ref), (mfin0_ref, mfin1_ref)

    strip_rows, strip_w = bias_ref.shape
    strip_origin = t + tq - strip_rows

    @pl.when((bi == 0) & (qi == 0))
    def _():
        r = lax.broadcasted_iota(jnp.int32, (strip_rows, HEAD_W), 0)
        u = lax.broadcasted_iota(jnp.int32, (strip_rows, HEAD_W), 1)
        rel = (r - u + strip_origin).astype(jnp.float32)

        def fill(a, _):
            u0 = pl.multiple_of(a * HEAD_W, HEAD_W)
            bias_ref[:, pl.ds(u0, HEAD_W)] = -c * jnp.abs(rel - u0.astype(jnp.float32))
            return 0

        lax.fori_loop(0, strip_w // HEAD_W, fill, 0)

    @pl.when(qi == 0)
    def _():
        vaug_ref[:, :HEAD_W] = v_ref[...]
        vaug_ref[:, HEAD_W:] = jnp.ones((t, HEAD_W), vaug_ref.dtype)

        def max_sq_norm(ref):
            x = ref[...].astype(jnp.float32)
            return jnp.max(jnp.sum(x * x, axis=1, keepdims=True), axis=0, keepdims=True)

        bound = jnp.sqrt(max_sq_norm(qall_ref) * max_sq_norm(k_ref)) * (
            DA ** -0.5 * LOG2E * 1.01)
        dist = jnp.minimum((2.0 * bound + ATTN_UNDERFLOW) / c, 2.0 * t)
        n_win = jnp.floor((2.0 * dist + (tq - 1)) / tk) + 2.0
        rem = n_win - jnp.floor(n_win / unroll) * unroll
        n_win += jnp.where((rem > 0) & (rem < unroll - 1), unroll - 1 - rem, 0.0)
        n_win = jnp.minimum(n_win, 1.0 * nk)
        win_ref[0] = n_win[0, 0].astype(jnp.int32)
        win_ref[1] = jnp.floor(dist)[0, 0].astype(jnp.int32)

    n_win, dist = win_ref[0], win_ref[1]

    def first_tile(start):
        return jnp.clip(jnp.maximum(start - dist, 0) // tk, 0, nk - n_win)

    lo_cur = first_tile(q0)
    lo_prev = first_tile(q0 - tq)

    def sweep1(cur, r):
        k0 = pl.multiple_of((lo_cur + r) * tk, tk)
        s = lax.dot_general(qq_ref[...], k_ref[pl.ds(k0, tk), :], nt,
                            preferred_element_type=jnp.float32)
        bias = [bias_ref[:, pl.ds(pl.multiple_of(k0 - q0 - rb * strip_rows + strip_origin,
                                                 HEAD_W), tk)]
                for rb in range(tq // strip_rows)]
        st = s + jnp.concatenate(bias + bias, axis=0)
        s_refs[cur][:, r * tk:(r + 1) * tk] = st
        part = functools.reduce(
            jnp.maximum, [st[:, a * HEAD_W:(a + 1) * HEAD_W] for a in range(tk // HEAD_W)])
        m_ref[...] = jnp.maximum(m_ref[...], part)

    def sweep2(prev, r):
        k0 = pl.multiple_of((lo_prev + r) * tk, tk)
        cols = slice(r * tk, (r + 1) * tk)
        for mp in range(2):
            rs = slice(mp * tq, (mp + 1) * tq)
            m_rep = mfin_refs[prev][rs, :]
            p = jnp.exp2(s_refs[prev][rs, cols] - jnp.concatenate([m_rep] * (tk // HEAD_W), axis=1))
            acc_ref[mp] += jnp.dot(p.astype(vaug_ref.dtype), vaug_ref[pl.ds(k0, tk), :],
                                   preferred_element_type=jnp.float32)

    def start_sweep1():
        q = q_ref[...].astype(jnp.float32) * (DA ** -0.5 * LOG2E)
        lane = lax.broadcasted_iota(jnp.int32, q.shape, 1)
        qq_ref[...] = jnp.concatenate([jnp.where(lane < DA, q, 0.0),
                                       jnp.where(lane >= DA, q, 0.0)],
                                      axis=0).astype(qq_ref.dtype)
        m_ref[...] = jnp.full(m_ref.shape, -jnp.inf, jnp.float32)

    def end_sweep1(cur):
        m = jnp.max(m_ref[...], axis=1, keepdims=True)
        mfin_refs[cur][...] = jnp.broadcast_to(m, (rows, HEAD_W))

    def start_sweep2():
        acc_ref[...] = jnp.zeros(acc_ref.shape, jnp.float32)

    def end_sweep2():
        lp = lam_ref[...]
        lam = (jnp.exp(jnp.sum(lp[0:1] * lp[1:2], axis=1, keepdims=True))
               - jnp.exp(jnp.sum(lp[2:3] * lp[3:4], axis=1, keepdims=True)) + lam_init)
        o = [acc_ref[mp][:, :HEAD_W] / acc_ref[mp][:, HEAD_W:] for mp in range(2)]
        o = o[0] - lam * o[1]
        o_ref[...] = (_rms(o, g_ref[...]) * (1.0 - lam_init)).astype(o_ref.dtype)

    def step(when, cur, prev):
        for n in (n for n in range(1, nk + 1) if n % unroll in (0, unroll - 1)):
            @pl.when(when & (n_win == n))
            def _():
                if cur is not None:
                    start_sweep1()
                if prev is not None:
                    start_sweep2()
                for r in range(n):
                    if prev is not None:
                        sweep2(prev, r)
                for r in range(n):
                    if cur is not None:
                        sweep1(cur, r)
                if cur is not None:
                    end_sweep1(cur)
                if prev is not None:
                    end_sweep2()

    step(qi == 0, 0, None)
    for cur in range(2):
        step((qi > 0) & (qi < nq) & (qi % 2 == cur), cur, 1 - cur)
    step(qi == nq, None, (nq - 1) % 2)


def _attention(proj, slopes, lam_p, subln_g, lam_init):
    b, t, _ = proj.shape
    tq, tk = _tile(t, ATTN_TQ), _tile(t, ATTN_TK)
    nq = t // tq
    strip_rows = min(tq, HEAD_W)
    h = N_HEADS
    return pl.pallas_call(
        functools.partial(_attn_kernel, tq=tq, tk=tk, lam_init=lam_init,
                          unroll=ATTN_UNROLL if (t // tk) % ATTN_UNROLL == 0 else 1),
        grid=(h, b, nq + 1),
        in_specs=[pl.BlockSpec((None, tq, HEAD_W),
                               lambda hi, bi, qi: (bi, jnp.minimum(qi, nq - 1), hi)),
                  pl.BlockSpec((None, t, HEAD_W), lambda hi, bi, qi: (bi, 0, hi)),
                  pl.BlockSpec((None, t, HEAD_W), lambda hi, bi, qi: (bi, 0, h + hi)),
                  pl.BlockSpec((None, t, HEAD_W), lambda hi, bi, qi: (bi, 0, 2 * h + hi)),
                  pl.BlockSpec((None, 1, HEAD_W), lambda hi, bi, qi: (hi, 0, 0)),
                  pl.BlockSpec(lam_p.shape, lambda hi, bi, qi: (0, 0)),
                  pl.BlockSpec((1, HEAD_W), lambda hi, bi, qi: (0, 0))],
        out_specs=pl.BlockSpec((None, tq, HEAD_W),
                               lambda hi, bi, qi: (bi, jnp.maximum(qi - 1, 0), hi)),
        out_shape=jax.ShapeDtypeStruct((b, t, h * HEAD_W), MXU_DTYPE),
        scratch_shapes=[pltpu.VMEM((2 * tq, t), jnp.float32),
                        pltpu.VMEM((2 * tq, t), jnp.float32),
                        pltpu.VMEM((2 * tq, HEAD_W), jnp.float32),
                        pltpu.VMEM((2 * tq, HEAD_W), jnp.float32),
                        pltpu.VMEM((2 * tq, HEAD_W), jnp.float32),
                        pltpu.VMEM((2, tq, 2 * HEAD_W), jnp.float32),
                        pltpu.VMEM((strip_rows, 2 * t + tq - strip_rows), jnp.float32),
                        pltpu.VMEM((t, 2 * HEAD_W), MXU_DTYPE),
                        pltpu.VMEM((2 * tq, HEAD_W), MXU_DTYPE),
                        pltpu.SMEM((2,), jnp.int32)],
        compiler_params=_params("parallel", "arbitrary", "arbitrary"),
    )(proj, proj, proj, proj, slopes, lam_p, subln_g.reshape(1, HEAD_W))


def _hgrn_constants(c):
    levels = []
    size = 2
    while size <= c:
        levels.append(size)
        size *= 2
    expo = np.zeros((len(levels) + 1, c, c), np.float32)
    mask = np.zeros((len(levels) + 1, c, c), np.float32)
    for li, size in enumerate(levels):
        half = size // 2
        for t in range(c):
            pos = t % size
            bd = t - pos + half - 1
            if pos >= half:
                expo[li, t, bd + 1:t + 1] = 1.0
                mask[li, t, t - pos:t - pos + half] = 1.0
            else:
                expo[li, t, t + 1:bd + 1] = 1.0
    expo[-1] = np.tril(np.ones((c, c), np.float32))
    mask[-1] = np.eye(c, dtype=np.float32)

    def pack(e):
        e = e[1:].reshape(-1, c)
        return np.concatenate([e, e, e], axis=1)

    return (np.stack([pack(expo), pack(expo[:, ::-1, ::-1])]),
            np.stack([mask, mask[:, ::-1, ::-1]]), len(levels))


def _split3(x):
    hi = x.astype(jnp.bfloat16)
    r1 = x - hi.astype(jnp.float32)
    mid = r1.astype(jnp.bfloat16)
    lo = (r1 - mid.astype(jnp.float32)).astype(jnp.bfloat16)
    return jnp.concatenate([hi, mid, lo], axis=0)


def _hgrn_kernel(q_ref, v_ref, gate_ref, zf_ref, zb_ref, lbf_ref, lbb_ref, gn_ref,
                 expo_ref, mask_ref, o_ref, of_ref, ob_ref, sf_ref, sb_ref,
                 *, chunk, group, n_levels, out_rows):
    t = q_ref.shape[0]
    c = chunk
    n_chunks = t // c
    nt = (((1,), (1,)), ((), ()))
    tn = (((0,), (0,)), ((), ()))
    sf_ref[...] = jnp.zeros_like(sf_ref)
    sb_ref[...] = jnp.zeros_like(sb_ref)

    def gates(r0, z_ref, lb_ref):
        z = z_ref[pl.ds(r0, c), :]
        lb = lb_ref[...]
        e = jnp.exp(-jnp.abs(z))
        r = 1.0 / (1.0 + e)
        er = e * r
        pos = z >= 0.0
        f = lb + (1.0 - lb) * jnp.where(pos, r, er)
        kk = (1.0 - lb) * jnp.where(pos, er, r)
        return kk, f, jnp.log(f)

    def exponents(gs, d):
        both = jnp.dot(expo_ref[d], jnp.concatenate([_split3(g) for g in gs], axis=1),
                       preferred_element_type=jnp.float32)
        return [both[:, i * HEAD_W:(i + 1) * HEAD_W] for i in range(len(gs))]

    def chunk_local(r0, kk, f, ex_all, d):
        rows = pl.ds(r0, c)
        n_mm = n_levels - 1
        b = ex_all[n_mm * c:]
        ex_all = jnp.exp(ex_all)
        odd = lax.broadcasted_iota(jnp.int32, f.shape, 0) % 2
        ex_pair = jnp.where(odd == 1 - d, f, 1.0)
        qf = q_ref[rows, :].astype(jnp.float32)
        v = v_ref[rows, :]
        att = mask_ref[d, n_levels] * lax.dot_general(
            qf.astype(MXU_DTYPE), kk.astype(MXU_DTYPE), nt,
            preferred_element_type=jnp.float32)
        for li in range(n_levels):
            ex = ex_pair if li == 0 else ex_all[(li - 1) * c:li * c]
            att += mask_ref[d, li] * lax.dot_general(
                (qf * ex).astype(MXU_DTYPE), (kk * ex).astype(MXU_DTYPE), nt,
                preferred_element_type=jnp.float32)
        intra = jnp.dot(att.astype(MXU_DTYPE), v, preferred_element_type=jnp.float32)
        q_dec = (qf * ex_all[n_mm * c:]).astype(MXU_DTYPE)
        b_last = b[c - 1:c] if d == 0 else b[0:1]
        kd = (kk * jnp.exp(b_last - b)).astype(MXU_DTYPE)
        update = lax.dot_general(v, kd, tn, preferred_element_type=jnp.float32)
        return rows, intra, q_dec, jnp.exp(b_last), update

    def direction(starts, z_ref, lb_ref, st_ref, out_ref, d):
        kks, fs, gs = zip(*[gates(r0, z_ref, lb_ref) for r0 in starts])
        exs = []
        for i in range(0, len(starts), 2):
            exs += exponents(gs[i:i + 2], d)
        local = [chunk_local(r0, kk, f, ex, d)
                 for r0, kk, f, ex in zip(starts, kks, fs, exs)]
        st = st_ref[...]
        for rows, intra, q_dec, decay, update in local:
            out_ref[rows, :] = intra + lax.dot_general(
                q_dec, st.astype(MXU_DTYPE), nt, preferred_element_type=jnp.float32)
            st = decay * st + update
        st_ref[...] = st

    def body(n, _):
        fwd = [pl.multiple_of((n * group + i) * c, c) for i in range(group)]
        bwd = [pl.multiple_of((n_chunks - 1 - n * group - i) * c, c) for i in range(group)]
        direction(fwd, zf_ref, lbf_ref, sf_ref, of_ref, 0)
        direction(bwd, zb_ref, lbb_ref, sb_ref, ob_ref, 1)
        return 0

    lax.fori_loop(0, n_chunks // group, body, 0)

    def finish(i, _):
        rows = pl.ds(pl.multiple_of(i * out_rows, out_rows), out_rows)
        o = _rms(of_ref[rows, :] + ob_ref[rows, :], gn_ref[...])
        gate = gate_ref[rows, :].astype(jnp.float32)
        o_ref[rows, :] = (o * (gate / (1.0 + jnp.exp(-gate)))).astype(o_ref.dtype)
        return 0

    lax.fori_loop(0, t // out_rows, finish, 0)


def _hgrn(proj, zgates, lb, gn):
    b, t, _ = proj.shape
    h = N_HEADS
    c = _tile(t, HG_CHUNK)
    expo, mask, n_levels = _hgrn_constants(c)
    expo = jnp.asarray(expo, MXU_DTYPE)
    mask = jnp.asarray(mask, jnp.float32)
    out_rows = _tile(t, 512)
    group = HG_GROUP if (t // c) % HG_GROUP == 0 else 1
    head_spec = lambda off: pl.BlockSpec((None, t, HEAD_W), lambda bi, hi: (bi, 0, off + hi))
    vec_spec = pl.BlockSpec((None, 1, HEAD_W), lambda bi, hi: (hi, 0, 0))
    return pl.pallas_call(
        functools.partial(_hgrn_kernel, chunk=c, group=group, n_levels=n_levels,
                          out_rows=out_rows),
        grid=(b, h),
        in_specs=[head_spec(3 * h), head_spec(4 * h), head_spec(5 * h),
                  head_spec(0), head_spec(h), vec_spec, vec_spec, vec_spec,
                  pl.BlockSpec(expo.shape, lambda bi, hi: (0, 0, 0)),
                  pl.BlockSpec(mask.shape, lambda bi, hi: (0, 0, 0, 0))],
        out_specs=pl.BlockSpec((None, t, HEAD_W), lambda bi, hi: (bi, 0, hi)),
        out_shape=jax.ShapeDtypeStruct((b, t, h * HEAD_W), MXU_DTYPE),
        scratch_shapes=[pltpu.VMEM((t, HEAD_W), jnp.float32),
                        pltpu.VMEM((t, HEAD_W), jnp.float32),
                        pltpu.VMEM((HEAD_W, HEAD_W), jnp.float32),
                        pltpu.VMEM((HEAD_W, HEAD_W), jnp.float32)],
        compiler_params=_params("parallel", "arbitrary"),
    )(proj, proj, proj, zgates, zgates,
      lb[0].reshape(h, 1, HEAD_W), lb[1].reshape(h, 1, HEAD_W), gn.reshape(h, 1, HEAD_W),
      expo, mask)


def _conv_inproj_kernel(x_ref, g_ref, w_ref, bg_ref, u_ref, hn_ref, *, cc):
    def block():
        r = jnp.dot(hn_ref[...], w_ref[...], preferred_element_type=jnp.float32)
        bg_ref[...] = r[:, :cc].astype(bg_ref.dtype)
        u_ref[...] = (r[:, cc:2 * cc] * r[:, 2 * cc:]).astype(u_ref.dtype)

    @pl.when(pl.program_id(1) == 0)
    def _():
        hn_ref[...] = _rms(x_ref[...], g_ref[...]).astype(hn_ref.dtype)
        block()

    @pl.when(pl.program_id(1) > 0)
    def _():
        block()


def _conv_inproj(x, g, w, tm=1024, cc=512):
    m, k = x.shape
    d = w.shape[1] // 3
    tm, cc = _tile(m, tm), _tile(d, cc)
    nb = d // cc
    w = w.reshape(k, 3, nb, cc).transpose(0, 2, 1, 3).reshape(k, 3 * d).astype(MXU_DTYPE)
    return pl.pallas_call(
        functools.partial(_conv_inproj_kernel, cc=cc),
        grid=(m // tm, nb),
        in_specs=[pl.BlockSpec((tm, k), lambda i, j: (i, 0)),
                  pl.BlockSpec((1, k), lambda i, j: (0, 0)),
                  pl.BlockSpec((k, 3 * cc), lambda i, j: (0, j))],
        out_specs=[pl.BlockSpec((tm, cc), lambda i, j: (i, j)),
                   pl.BlockSpec((tm, cc), lambda i, j: (i, j))],
        out_shape=[jax.ShapeDtypeStruct((m, d), MXU_DTYPE),
                   jax.ShapeDtypeStruct((m, d), MXU_DTYPE)],
        scratch_shapes=[pltpu.VMEM((tm, k), MXU_DTYPE)],
        compiler_params=_params("parallel", "arbitrary"),
    )(x, g.reshape(1, k), w)


def _conv_kernel(bg_ref, u_ref, up_ref, un_ref, cw_ref, w_ref, r_ref, o_ref, us_ref,
                 *, tm, cw, tiles_per_seq):
    i = pl.program_id(0)
    first = (i % tiles_per_seq) == 0
    last = (i % tiles_per_seq) == tiles_per_seq - 1
    d = u_ref.shape[1]
    ys = []
    for n in range(d // cw):
        cs = slice(n * cw, (n + 1) * cw)
        u = u_ref[:, cs].astype(jnp.float32)
        us_ref[n, 8:tm + 8, :] = u
        us_ref[n, 7:8, :] = jnp.where(
            first, 0.0, up_ref[CONV_HALO - 1:CONV_HALO, cs].astype(jnp.float32))
        us_ref[n, tm + 8:tm + 9, :] = jnp.where(
            last, 0.0, un_ref[0:1, cs].astype(jnp.float32))
        w = cw_ref[:, cs]
        conv = (w[0:1] * us_ref[n, 7:tm + 7, :] + w[1:2] * u
                + w[2:3] * us_ref[n, 9:tm + 9, :])
        ys.append((bg_ref[:, cs].astype(jnp.float32) * conv).astype(w_ref.dtype))
    o_ref[...] = r_ref[...] + jnp.dot(jnp.concatenate(ys, axis=1), w_ref[...],
                                      preferred_element_type=jnp.float32)


def _conv_mixer(bg, u, conv_w, w_out, res, seq, tm=512, cw=256):
    m, d = u.shape
    n = w_out.shape[1]
    tm, tn, cw = _tile(seq, tm), n, _tile(d, cw)
    tiles_per_seq = seq // tm
    hb = tm // CONV_HALO
    n_hb = m // CONV_HALO
    main = pl.BlockSpec((tm, d), lambda i, j: (i, 0))
    prev = pl.BlockSpec((CONV_HALO, d), lambda i, j: (jnp.maximum(i * hb - 1, 0), 0))
    nxt = pl.BlockSpec((CONV_HALO, d), lambda i, j: (jnp.minimum((i + 1) * hb, n_hb - 1), 0))
    return pl.pallas_call(
        functools.partial(_conv_kernel, tm=tm, cw=cw, tiles_per_seq=tiles_per_seq),
        grid=(m // tm, n // tn),
        in_specs=[main, main, prev, nxt,
                  pl.BlockSpec(conv_w.shape, lambda i, j: (0, 0)),
                  pl.BlockSpec((d, tn), lambda i, j: (0, j)),
                  pl.BlockSpec((tm, tn), lambda i, j: (i, j))],
        out_specs=pl.BlockSpec((tm, tn), lambda i, j: (i, j)),
        out_shape=jax.ShapeDtypeStruct((m, n), jnp.float32),
        scratch_shapes=[pltpu.VMEM((d // cw, tm + 16, cw), jnp.float32)],
        compiler_params=_params("parallel", "arbitrary"),
    )(bg, u, u, u, conv_w, w_out, res)


def kernel(x, mix_norm_g, mlp_norm_g, final_norm_g, ab_w_in, ab_w_out, diff_lambda,
           diff_subln_g, hgrn_lb, hgrn_norm_g, conv_w_in, conv_w, conv_w_out,
           mlp_w_up, mlp_w_down):
    bsz, seq, d = x.shape
    m = bsz * seq
    h = N_HEADS
    w_attn = h * HEAD_W
    n_bf = 6 * w_attn
    cast = lambda w: w.astype(MXU_DTYPE)

    hres = x.reshape(m, d)

    proj, zgates = _norm_matmul_split(hres, mix_norm_g[0], cast(ab_w_in[0]), n_bf,
                                      MXU_DTYPE, jnp.float32)
    proj = proj.reshape(bsz, seq, n_bf)
    zgates = zgates.reshape(bsz, seq, 2 * w_attn)

    lam_init = 0.8 - 0.6 * math.exp(-0.3 * 0)
    slopes = 2.0 ** (-8.0 * jnp.arange(1, h + 1, dtype=jnp.float32) / h)
    slopes = jnp.broadcast_to(slopes[:, None, None], (h, 1, HEAD_W))
    oa = _attention(proj, slopes, diff_lambda[0].astype(jnp.float32), diff_subln_g[0], lam_init)

    lb = jnp.cumsum(jax.nn.softmax(hgrn_lb.astype(jnp.float32), axis=1), axis=1)[:, 0]
    ob = _hgrn(proj, zgates, lb, hgrn_norm_g[0])

    hres = _proj2_residual(oa.reshape(m, w_attn), ob.reshape(m, w_attn),
                           cast(ab_w_out[0]), hres)
    hres = _mlp(hres, mlp_norm_g[0], cast(mlp_w_up[0]), cast(mlp_w_down[0]),
                final_norm_g, False)

    bgate, u = _conv_inproj(hres, mix_norm_g[1], conv_w_in[0])
    hres = _conv_mixer(bgate, u, conv_w[0], cast(conv_w_out[0]), hres, seq)
    hres = _mlp(hres, mlp_norm_g[1], cast(mlp_w_up[1]), cast(mlp_w_down[1]),
                final_norm_g, True)
    return hres.reshape(bsz, seq, d)
```

```python
import functools
import math

import numpy as np
import jax
import jax.numpy as jnp
from jax import lax
from jax.experimental import pallas as pl
from jax.experimental.pallas import tpu as pltpu

EPS = 1e-6
N_HEADS = 8
HEAD_W = 128
DA = 64
HG_CHUNK = 64
HG_GROUP = 8
CONV_HALO = 16
MXU_DTYPE = jnp.bfloat16
LOG2E = 1.4426950408889634
ATTN_TQ = 256
ATTN_TK = 512
ATTN_UNROLL = 4
ATTN_UNDERFLOW = 160.0
VMEM_LIMIT = 56 * 1024 * 1024


def _tile(dim, pref):
    return pref if dim % pref == 0 else dim


def _rms(x, g):
    ms = jnp.mean(x * x, axis=-1, keepdims=True)
    return x * lax.rsqrt(ms + EPS) * g


def _params(*sem):
    return pltpu.CompilerParams(dimension_semantics=sem, vmem_limit_bytes=VMEM_LIMIT)


def _norm_matmul_split_kernel(x_ref, g_ref, w_ref, lo_ref, hi_ref, hn_ref, *, n_lo):
    j = pl.program_id(1)

    def product():
        return jnp.dot(hn_ref[...], w_ref[...], preferred_element_type=jnp.float32)

    @pl.when(j == 0)
    def _():
        hn_ref[...] = _rms(x_ref[...], g_ref[...]).astype(hn_ref.dtype)
        lo_ref[...] = product().astype(lo_ref.dtype)

    @pl.when((j > 0) & (j < n_lo))
    def _():
        lo_ref[...] = product().astype(lo_ref.dtype)

    @pl.when(j >= n_lo)
    def _():
        hi_ref[...] = product().astype(hi_ref.dtype)


def _norm_matmul_split(x, g, w, n_split, lo_dtype, hi_dtype, tm=1024, tn=1024):
    m, k = x.shape
    n = w.shape[1]
    tm, tn = _tile(m, tm), _tile(n_split, tn)
    assert n_split % tn == 0 and (n - n_split) % tn == 0
    n_lo = n_split // tn
    return pl.pallas_call(
        functools.partial(_norm_matmul_split_kernel, n_lo=n_lo),
        grid=(m // tm, n // tn),
        in_specs=[pl.BlockSpec((tm, k), lambda i, j: (i, 0)),
                  pl.BlockSpec((1, k), lambda i, j: (0, 0)),
                  pl.BlockSpec((k, tn), lambda i, j: (0, j))],
        out_specs=[pl.BlockSpec((tm, tn), lambda i, j: (i, jnp.minimum(j, n_lo - 1))),
                   pl.BlockSpec((tm, tn), lambda i, j: (i, jnp.maximum(j - n_lo, 0)))],
        out_shape=[jax.ShapeDtypeStruct((m, n_split), lo_dtype),
                   jax.ShapeDtypeStruct((m, n - n_split), hi_dtype)],
        scratch_shapes=[pltpu.VMEM((tm, k), MXU_DTYPE)],
        compiler_params=_params("parallel", "arbitrary"),
    )(x, g.reshape(1, k), w)


def _proj2_kernel(a_ref, b_ref, wa_ref, wb_ref, r_ref, o_ref):
    acc = jnp.dot(a_ref[...], wa_ref[...], preferred_element_type=jnp.float32)
    acc += jnp.dot(b_ref[...], wb_ref[...], preferred_element_type=jnp.float32)
    o_ref[...] = r_ref[...] + acc


def _proj2_residual(a, b, w, res, tm=512, tn=2048):
    m, ka = a.shape
    kb = b.shape[1]
    n = w.shape[1]
    assert ka == kb and w.shape[0] == ka + kb
    tm, tn = _tile(m, tm), _tile(n, tn)
    return pl.pallas_call(
        _proj2_kernel,
        grid=(m // tm, n // tn),
        in_specs=[pl.BlockSpec((tm, ka), lambda i, j: (i, 0)),
                  pl.BlockSpec((tm, kb), lambda i, j: (i, 0)),
                  pl.BlockSpec((ka, tn), lambda i, j: (0, j)),
                  pl.BlockSpec((kb, tn), lambda i, j: (1, j)),
                  pl.BlockSpec((tm, tn), lambda i, j: (i, j))],
        out_specs=pl.BlockSpec((tm, tn), lambda i, j: (i, j)),
        out_shape=jax.ShapeDtypeStruct((m, n), jnp.float32),
        compiler_params=_params("parallel", "arbitrary"),
    )(a, b, w, w, res)


def _mlp_kernel(h_ref, g_ref, wu_ref, wd_ref, gf_ref, o_ref, hn_ref, *, final_norm):
    j = pl.program_id(1)

    def chunk():
        a = jnp.dot(hn_ref[...], wu_ref[...], preferred_element_type=jnp.float32)
        a = jnp.maximum(a, 0.0)
        a = (a * a).astype(wd_ref.dtype)
        return jnp.dot(a, wd_ref[...], preferred_element_type=jnp.float32)

    @pl.when(j == 0)
    def _():
        hn_ref[...] = _rms(h_ref[...], g_ref[...]).astype(hn_ref.dtype)
        o_ref[...] = h_ref[...] + chunk()

    @pl.when(j > 0)
    def _():
        o_ref[...] += chunk()

    if final_norm:
        @pl.when(j == pl.num_programs(1) - 1)
        def _():
            o_ref[...] = _rms(o_ref[...], gf_ref[...])


def _mlp(h, g, w_up, w_down, g_final, final_norm, tm=1024, tf=512):
    m, d = h.shape
    f = w_up.shape[1]
    tm, tf = _tile(m, tm), _tile(f, tf)
    return pl.pallas_call(
        functools.partial(_mlp_kernel, final_norm=final_norm),
        grid=(m // tm, f // tf),
        in_specs=[pl.BlockSpec((tm, d), lambda i, j: (i, 0)),
                  pl.BlockSpec((1, d), lambda i, j: (0, 0)),
                  pl.BlockSpec((d, tf), lambda i, j: (0, j)),
                  pl.BlockSpec((tf, d), lambda i, j: (j, 0)),
                  pl.BlockSpec((1, d), lambda i, j: (0, 0))],
        out_specs=pl.BlockSpec((tm, d), lambda i, j: (i, 0)),
        out_shape=jax.ShapeDtypeStruct((m, d), jnp.float32),
        scratch_shapes=[pltpu.VMEM((tm, d), MXU_DTYPE)],
        compiler_params=_params("parallel", "arbitrary"),
    )(h, g.reshape(1, d), w_up, w_down, g_final.reshape(1, d))


def _attn_kernel(q_ref, qall_ref, k_ref, v_ref, slope_ref, lam_ref, g_ref, o_ref,
                 s0_ref, s1_ref, m_ref, mfin0_ref, mfin1_ref, acc_ref, bias_ref, vaug_ref,
                 qq_ref, win_ref, *, tq, tk, unroll, lam_init):
    t = k_ref.shape[0]
    nk, nq = t // tk, t // tq
    rows = 2 * tq
    bi, qi = pl.program_id(1), pl.program_id(2)
    q0 = qi * tq
    nt = (((1,), (1,)), ((), ()))
    c = slope_ref[...][:, :1] * LOG2E
    s_refs, mfin_refs = (s0_ref, s1_ref), (mfin0_ref, mfin1_ref)

    strip_rows, strip_w = bias_ref.shape
    strip_origin = t + tq - strip_rows

    @pl.when((bi == 0) & (qi == 0))
    def _():
        r = lax.broadcasted_iota(jnp.int32, (strip_rows, HEAD_W), 0)
        u = lax.broadcasted_iota(jnp.int32, (strip_rows, HEAD_W), 1)
        rel = (r - u + strip_origin).astype(jnp.float32)

        def fill(a, _):
            u0 = pl.multiple_of(a * HEAD_W, HEAD_W)
            bias_ref[:, pl.ds(u0, HEAD_W)] = -c * jnp.abs(rel - u0.astype(jnp.float32))
            return 0

        lax.fori_loop(0, strip_w // HEAD_W, fill, 0)

    @pl.when(qi == 0)
    def _():
        vaug_ref[:, :HEAD_W] = v_ref[...]
        vaug_ref[:, HEAD_W:] = jnp.ones((t, HEAD_W), vaug_ref.dtype)

        def max_sq_norm(ref):
            x = ref[...].astype(jnp.float32)
            return jnp.max(jnp.sum(x * x, axis=1, keepdims=True), axis=0, keepdims=True)

        bound = jnp.sqrt(max_sq_norm(qall_ref) * max_sq_norm(k_ref)) * (
            DA ** -0.5 * LOG2E * 1.01)
        dist = jnp.minimum((2.0 * bound + ATTN_UNDERFLOW) / c, 2.0 * t)
        n_win = jnp.floor((2.0 * dist + (tq - 1)) / tk) + 2.0
        rem = n_win - jnp.floor(n_win / unroll) * unroll
        n_win += jnp.where((rem > 0) & (rem < unroll - 1), unroll - 1 - rem, 0.0)
        n_win = jnp.minimum(n_win, 1.0 * nk)
        win_ref[0] = n_win[0, 0].astype(jnp.int32)
        win_ref[1] = jnp.floor(dist)[0, 0].astype(jnp.int32)

    n_win, dist = win_ref[0], win_ref[1]

    def first_tile(start):
        return jnp.clip(jnp.maximum(start - dist, 0) // tk, 0, nk - n_win)

    lo_cur = first_tile(q0)
    lo_prev = first_tile(q0 - tq)

    def sweep1(cur, r):
        k0 = pl.multiple_of((lo_cur + r) * tk, tk)
        s = lax.dot_general(qq_ref[...], k_ref[pl.ds(k0, tk), :], nt,
                            preferred_element_type=jnp.float32)
        bias = [bias_ref[:, pl.ds(pl.multiple_of(k0 - q0 - rb * strip_rows + strip_origin,
                                                 HEAD_W), tk)]
                for rb in range(tq // strip_rows)]
        st = s + jnp.concatenate(bias + bias, axis=0)
        s_refs[cur][:, r * tk:(r + 1) * tk] = st
        part = functools.reduce(
            jnp.maximum, [st[:, a * HEAD_W:(a + 1) * HEAD_W] for a in range(tk // HEAD_W)])
        m_ref[...] = jnp.maximum(m_ref[...], part)

    def sweep2(prev, r):
        k0 = pl.multiple_of((lo_prev + r) * tk, tk)
        cols = slice(r * tk, (r + 1) * tk)
        for mp in range(2):
            rs = slice(mp * tq, (mp + 1) * tq)
            m_rep = mfin_refs[prev][rs, :]
            p = jnp.exp2(s_refs[prev][rs, cols] - jnp.concatenate([m_rep] * (tk // HEAD_W), axis=1))
            acc_ref[mp] += jnp.dot(p.astype(vaug_ref.dtype), vaug_ref[pl.ds(k0, tk), :],
                                   preferred_element_type=jnp.float32)

    def start_sweep1():
        q = q_ref[...].astype(jnp.float32) * (DA ** -0.5 * LOG2E)
        lane = lax.broadcasted_iota(jnp.int32, q.shape, 1)
        qq_ref[...] = jnp.concatenate([jnp.where(lane < DA, q, 0.0),
                                       jnp.where(lane >= DA, q, 0.0)],
                                      axis=0).astype(qq_ref.dtype)
        m_ref[...] = jnp.full(m_ref.shape, -jnp.inf, jnp.float32)

    def end_sweep1(cur):
        m = jnp.max(m_ref[...], axis=1, keepdims=True)
        mfin_refs[cur][...] = jnp.broadcast_to(m, (rows, HEAD_W))

    def start_sweep2():
        acc_ref[...] = jnp.zeros(acc_ref.shape, jnp.float32)

    def end_sweep2():
        lp = lam_ref[...]
        lam = (jnp.exp(jnp.sum(lp[0:1] * lp[1:2], axis=1, keepdims=True))
               - jnp.exp(jnp.sum(lp[2:3] * lp[3:4], axis=1, keepdims=True)) + lam_init)
        o = [acc_ref[mp][:, :HEAD_W] / acc_ref[mp][:, HEAD_W:] for mp in range(2)]
        o = o[0] - lam * o[1]
        o_ref[...] = (_rms(o, g_ref[...]) * (1.0 - lam_init)).astype(o_ref.dtype)

    def step(when, cur, prev):
        for n in (n for n in range(1, nk + 1) if n % unroll in (0, unroll - 1)):
            @pl.when(when & (n_win == n))
            def _():
                if cur is not None:
                    start_sweep1()
                if prev is not None:
                    start_sweep2()
                for r in range(n):
                    if prev is not None:
                        sweep2(prev, r)
                for r in range(n):
                    if cur is not None:
                        sweep1(cur, r)
                if cur is not None:
                    end_sweep1(cur)
                if prev is not None:
                    end_sweep2()

    step(qi == 0, 0, None)
    for cur in range(2):
        step((qi > 0) & (qi < nq) & (qi % 2 == cur), cur, 1 - cur)
    step(qi == nq, None, (nq - 1) % 2)


def _attention(proj, slopes, lam_p, subln_g, lam_init):
    b, t, _ = proj.shape
    tq, tk = _tile(t, ATTN_TQ), _tile(t, ATTN_TK)
    nq = t // tq
    strip_rows = min(tq, HEAD_W)
    h = N_HEADS
    return pl.pallas_call(
        functools.partial(_attn_kernel, tq=tq, tk=tk, lam_init=lam_init,
                          unroll=ATTN_UNROLL if (t // tk) % ATTN_UNROLL == 0 else 1),
        grid=(h, b, nq + 1),
        in_specs=[pl.BlockSpec((None, tq, HEAD_W),
                               lambda hi, bi, qi: (bi, jnp.minimum(qi, nq - 1), hi)),
                  pl.BlockSpec((None, t, HEAD_W), lambda hi, bi, qi: (bi, 0, hi)),
                  pl.BlockSpec((None, t, HEAD_W), lambda hi, bi, qi: (bi, 0, h + hi)),
                  pl.BlockSpec((None, t, HEAD_W), lambda hi, bi, qi: (bi, 0, 2 * h + hi)),
                  pl.BlockSpec((None, 1, HEAD_W), lambda hi, bi, qi: (hi, 0, 0)),
                  pl.BlockSpec(lam_p.shape, lambda hi, bi, qi: (0, 0)),
                  pl.BlockSpec((1, HEAD_W), lambda hi, bi, qi: (0, 0))],
        out_specs=pl.BlockSpec((None, tq, HEAD_W),
                               lambda hi, bi, qi: (bi, jnp.maximum(qi - 1, 0), hi)),
        out_shape=jax.ShapeDtypeStruct((b, t, h * HEAD_W), MXU_DTYPE),
        scratch_shapes=[pltpu.VMEM((2 * tq, t), jnp.float32),
                        pltpu.VMEM((2 * tq, t), jnp.float32),
                        pltpu.VMEM((2 * tq, HEAD_W), jnp.float32),
                        pltpu.VMEM((2 * tq, HEAD_W), jnp.float32),
                        pltpu.VMEM((2 * tq, HEAD_W), jnp.float32),
                        pltpu.VMEM((2, tq, 2 * HEAD_W), jnp.float32),
                        pltpu.VMEM((strip_rows, 2 * t + tq - strip_rows), jnp.float32),
                        pltpu.VMEM((t, 2 * HEAD_W), MXU_DTYPE),
                        pltpu.VMEM((2 * tq, HEAD_W), MXU_DTYPE),
                        pltpu.SMEM((2,), jnp.int32)],
        compiler_params=_params("parallel", "arbitrary", "arbitrary"),
    )(proj, proj, proj, proj, slopes, lam_p, subln_g.reshape(1, HEAD_W))


def _hgrn_constants(c):
    levels = []
    size = 2
    while size <= c:
        levels.append(size)
        size *= 2
    expo = np.zeros((len(levels) + 1, c, c), np.float32)
    mask = np.zeros((len(levels) + 1, c, c), np.float32)
    for li, size in enumerate(levels):
        half = size // 2
        for t in range(c):
            pos = t % size
            bd = t - pos + half - 1
            if pos >= half:
                expo[li, t, bd + 1:t + 1] = 1.0
                mask[li, t, t - pos:t - pos + half] = 1.0
            else:
                expo[li, t, t + 1:bd + 1] = 1.0
    expo[-1] = np.tril(np.ones((c, c), np.float32))
    mask[-1] = np.eye(c, dtype=np.float32)

    def pack(e):
        e = e[1:].reshape(-1, c)
        return np.concatenate([e, e, e], axis=1)

    return (np.stack([pack(expo), pack(expo[:, ::-1, ::-1])]),
            np.stack([mask, mask[:, ::-1, ::-1]]), len(levels))


def _split3(x):
    hi = x.astype(jnp.bfloat16)
    r1 = x - hi.astype(jnp.float32)
    mid = r1.astype(jnp.bfloat16)
    lo = (r1 - mid.astype(jnp.float32)).astype(jnp.bfloat16)
    return jnp.concatenate([hi, mid, lo], axis=0)


def _hgrn_kernel(q_ref, v_ref, gate_ref, zf_ref, zb_ref, lbf_ref, lbb_ref, gn_ref,
                 expo_ref, mask_ref, o_ref, of_ref, ob_ref, sf_ref, sb_ref,
                 *, chunk, group, n_levels, out_rows):
    t = q_ref.shape[0]
    c = chunk
    n_chunks = t // c
    nt = (((1,), (1,)), ((), ()))
    tn = (((0,), (0,)), ((), ()))
    sf_ref[...] = jnp.zeros_like(sf_ref)
    sb_ref[...] = jnp.zeros_like(sb_ref)

    def gates(r0, z_ref, lb_ref):
        z = z_ref[pl.ds(r0, c), :]
        lb = lb_ref[...]
        e = jnp.exp(-jnp.abs(z))
        r = 1.0 / (1.0 + e)
        er = e * r
        pos = z >= 0.0
        f = lb + (1.0 - lb) * jnp.where(pos, r, er)
        kk = (1.0 - lb) * jnp.where(pos, er, r)
        return kk, f, jnp.log(f)

    def exponents(gs, d):
        both = jnp.dot(expo_ref[d], jnp.concatenate([_split3(g) for g in gs], axis=1),
                       preferred_element_type=jnp.float32)
        return [both[:, i * HEAD_W:(i + 1) * HEAD_W] for i in range(len(gs))]

    def chunk_local(r0, kk, f, ex_all, d):
        rows = pl.ds(r0, c)
        n_mm = n_levels - 1
        b = ex_all[n_mm * c:]
        ex_all = jnp.exp(ex_all)
        odd = lax.broadcasted_iota(jnp.int32, f.shape, 0) % 2
        ex_pair = jnp.where(odd == 1 - d, f, 1.0)
        qf = q_ref[rows, :].astype(jnp.float32)
        v = v_ref[rows, :]
        att = mask_ref[d, n_levels] * lax.dot_general(
            qf.astype(MXU_DTYPE), kk.astype(MXU_DTYPE), nt,
            preferred_element_type=jnp.float32)
        for li in range(n_levels):
            ex = ex_pair if li == 0 else ex_all[(li - 1) * c:li * c]
            att += mask_ref[d, li] * lax.dot_general(
                (qf * ex).astype(MXU_DTYPE), (kk * ex).astype(MXU_DTYPE), nt,
                preferred_element_type=jnp.float32)
        intra = jnp.dot(att.astype(MXU_DTYPE), v, preferred_element_type=jnp.float32)
        q_dec = (qf * ex_all[n_mm * c:]).astype(MXU_DTYPE)
        b_last = b[c - 1:c] if d == 0 else b[0:1]
        kd = (kk * jnp.exp(b_last - b)).astype(MXU_DTYPE)
        update = lax.dot_general(v, kd, tn, preferred_element_type=jnp.float32)
        return rows, intra, q_dec, jnp.exp(b_last), update

    def direction(starts, z_ref, lb_ref, st_ref, out_ref, d):
        kks, fs, gs = zip(*[gates(r0, z_ref, lb_ref) for r0 in starts])
        exs = []
        for i in range(0, len(starts), 2):
            exs += exponents(gs[i:i + 2], d)
        local = [chunk_local(r0, kk, f, ex, d)
                 for r0, kk, f, ex in zip(starts, kks, fs, exs)]
        st = st_ref[...]
        for rows, intra, q_dec, decay, update in local:
            out_ref[rows, :] = intra + lax.dot_general(
                q_dec, st.astype(MXU_DTYPE), nt, preferred_element_type=jnp.float32)
            st = decay * st + update
        st_ref[...] = st

    def body(n, _):
        fwd = [pl.multiple_of((n * group + i) * c, c) for i in range(group)]
        bwd = [pl.multiple_of((n_chunks - 1 - n * group - i) * c, c) for i in range(group)]
        direction(fwd, zf_ref, lbf_ref, sf_ref, of_ref, 0)
        direction(bwd, zb_ref, lbb_ref, sb_ref, ob_ref, 1)
        return 0

    lax.fori_loop(0, n_chunks // group, body, 0)

    def finish(i, _):
        rows = pl.ds(pl.multiple_of(i * out_rows, out_rows), out_rows)
        o = _rms(of_ref[rows, :] + ob_ref[rows, :], gn_ref[...])
        gate = gate_ref[rows, :].astype(jnp.float32)
        o_ref[rows, :] = (o * (gate / (1.0 + jnp.exp(-gate)))).astype(o_ref.dtype)
        return 0

    lax.fori_loop(0, t // out_rows, finish, 0)


def _hgrn(proj, zgates, lb, gn):
    b, t, _ = proj.shape
    h = N_HEADS
    c = _tile(t, HG_CHUNK)
    expo, mask, n_levels = _hgrn_constants(c)
    expo = jnp.asarray(expo, MXU_DTYPE)
    mask = jnp.asarray(mask, jnp.float32)
    out_rows = _tile(t, 512)
    group = HG_GROUP if (t // c) % HG_GROUP == 0 else 1
    head_spec = lambda off: pl.BlockSpec((None, t, HEAD_W), lambda bi, hi: (bi, 0, off + hi))
    vec_spec = pl.BlockSpec((None, 1, HEAD_W), lambda bi, hi: (hi, 0, 0))
    return pl.pallas_call(
        functools.partial(_hgrn_kernel, chunk=c, group=group, n_levels=n_levels,
                          out_rows=out_rows),
        grid=(b, h),
        in_specs=[head_spec(3 * h), head_spec(4 * h), head_spec(5 * h),
                  head_spec(0), head_spec(h), vec_spec, vec_spec, vec_spec,
                  pl.BlockSpec(expo.shape, lambda bi, hi: (0, 0, 0)),
                  pl.BlockSpec(mask.shape, lambda bi, hi: (0, 0, 0, 0))],
        out_specs=pl.BlockSpec((None, t, HEAD_W), lambda bi, hi: (bi, 0, hi)),
        out_shape=jax.ShapeDtypeStruct((b, t, h * HEAD_W), MXU_DTYPE),
        scratch_shapes=[pltpu.VMEM((t, HEAD_W), jnp.float32),
                        pltpu.VMEM((t, HEAD_W), jnp.float32),
                        pltpu.VMEM((HEAD_W, HEAD_W), jnp.float32),
                        pltpu.VMEM((HEAD_W, HEAD_W), jnp.float32)],
        compiler_params=_params("parallel", "arbitrary"),
    )(proj, proj, proj, zgates, zgates,
      lb[0].reshape(h, 1, HEAD_W), lb[1].reshape(h, 1, HEAD_W), gn.reshape(h, 1, HEAD_W),
      expo, mask)


def _conv_inproj_kernel(x_ref, g_ref, w_ref, bg_ref, u_ref, hn_ref, *, cc):
    def block():
        r = jnp.dot(hn_ref[...], w_ref[...], preferred_element_type=jnp.float32)
        bg_ref[...] = r[:, :cc].astype(bg_ref.dtype)
        u_ref[...] = (r[:, cc:2 * cc] * r[:, 2 * cc:]).astype(u_ref.dtype)

    @pl.when(pl.program_id(1) == 0)
    def _():
        hn_ref[...] = _rms(x_ref[...], g_ref[...]).astype(hn_ref.dtype)
        block()

    @pl.when(pl.program_id(1) > 0)
    def _():
        block()


def _conv_inproj(x, g, w, tm=1024, cc=512):
    m, k = x.shape
    d = w.shape[1] // 3
    tm, cc = _tile(m, tm), _tile(d, cc)
    nb = d // cc
    w = w.reshape(k, 3, nb, cc).transpose(0, 2, 1, 3).reshape(k, 3 * d).astype(MXU_DTYPE)
    return pl.pallas_call(
        functools.partial(_conv_inproj_kernel, cc=cc),
        grid=(m // tm, nb),
        in_specs=[pl.BlockSpec((tm, k), lambda i, j: (i, 0)),
                  pl.BlockSpec((1, k), lambda i, j: (0, 0)),
                  pl.BlockSpec((k, 3 * cc), lambda i, j: (0, j))],
        out_specs=[pl.BlockSpec((tm, cc), lambda i, j: (i, j)),
                   pl.BlockSpec((tm, cc), lambda i, j: (i, j))],
        out_shape=[jax.ShapeDtypeStruct((m, d), MXU_DTYPE),
                   jax.ShapeDtypeStruct((m, d), MXU_DTYPE)],
        scratch_shapes=[pltpu.VMEM((tm, k), MXU_DTYPE)],
        compiler_params=_params("parallel", "arbitrary"),
    )(x, g.reshape(1, k), w)


def _conv_kernel(bg_ref, u_ref, up_ref, un_ref, cw_ref, w_ref, r_ref, o_ref,
                 *, tm, cw, tiles_per_seq):
    i = pl.program_id(0)
    first = (i % tiles_per_seq) == 0
    last = (i % tiles_per_seq) == tiles_per_seq - 1
    d = u_ref.shape[1]
    ys = []
    row = lax.broadcasted_iota(jnp.int32, (tm, cw), 0)
    for n in range(d // cw):
        cs = slice(n * cw, (n + 1) * cw)
        u = u_ref[:, cs].astype(jnp.float32)
        before = jnp.where(first, 0.0, up_ref[CONV_HALO - 1:CONV_HALO, cs].astype(jnp.float32))
        after = jnp.where(last, 0.0, un_ref[0:1, cs].astype(jnp.float32))
        u_prev = jnp.where(row == 0, before, pltpu.roll(u, 1, 0))
        u_next = jnp.where(row == tm - 1, after, pltpu.roll(u, tm - 1, 0))
        w = cw_ref[:, cs]
        conv = w[0:1] * u_prev + w[1:2] * u + w[2:3] * u_next
        ys.append((bg_ref[:, cs].astype(jnp.float32) * conv).astype(w_ref.dtype))
    o_ref[...] = r_ref[...] + jnp.dot(jnp.concatenate(ys, axis=1), w_ref[...],
                                      preferred_element_type=jnp.float32)


def _conv_mixer(bg, u, conv_w, w_out, res, seq, tm=512, cw=256):
    m, d = u.shape
    n = w_out.shape[1]
    tm, tn, cw = _tile(seq, tm), n, _tile(d, cw)
    tiles_per_seq = seq // tm
    hb = tm // CONV_HALO
    n_hb = m // CONV_HALO
    main = pl.BlockSpec((tm, d), lambda i, j: (i, 0))
    prev = pl.BlockSpec((CONV_HALO, d), lambda i, j: (jnp.maximum(i * hb - 1, 0), 0))
    nxt = pl.BlockSpec((CONV_HALO, d), lambda i, j: (jnp.minimum((i + 1) * hb, n_hb - 1), 0))
    return pl.pallas_call(
        functools.partial(_conv_kernel, tm=tm, cw=cw, tiles_per_seq=tiles_per_seq),
        grid=(m // tm, n // tn),
        in_specs=[main, main, prev, nxt,
                  pl.BlockSpec(conv_w.shape, lambda i, j: (0, 0)),
                  pl.BlockSpec((d, tn), lambda i, j: (0, j)),
                  pl.BlockSpec((tm, tn), lambda i, j: (i, j))],
        out_specs=pl.BlockSpec((tm, tn), lambda i, j: (i, j)),
        out_shape=jax.ShapeDtypeStruct((m, n), jnp.float32),
        compiler_params=_params("parallel", "arbitrary"),
    )(bg, u, u, u, conv_w, w_out, res)


def kernel(x, mix_norm_g, mlp_norm_g, final_norm_g, ab_w_in, ab_w_out, diff_lambda,
           diff_subln_g, hgrn_lb, hgrn_norm_g, conv_w_in, conv_w, conv_w_out,
           mlp_w_up, mlp_w_down):
    bsz, seq, d = x.shape
    m = bsz * seq
    h = N_HEADS
    w_attn = h * HEAD_W
    n_bf = 6 * w_attn
    cast = lambda w: w.astype(MXU_DTYPE)

    hres = x.reshape(m, d)

    proj, zgates = _norm_matmul_split(hres, mix_norm_g[0], cast(ab_w_in[0]), n_bf,
                                      MXU_DTYPE, jnp.float32)
    proj = proj.reshape(bsz, seq, n_bf)
    zgates = zgates.reshape(bsz, seq, 2 * w_attn)

    lam_init = 0.8 - 0.6 * math.exp(-0.3 * 0)
    slopes = 2.0 ** (-8.0 * jnp.arange(1, h + 1, dtype=jnp.float32) / h)
    slopes = jnp.broadcast_to(slopes[:, None, None], (h, 1, HEAD_W))
    oa = _attention(proj, slopes, diff_lambda[0].astype(jnp.float32), diff_subln_g[0], lam_init)

    lb = jnp.cumsum(jax.nn.softmax(hgrn_lb.astype(jnp.float32), axis=1), axis=1)[:, 0]
    ob = _hgrn(proj, zgates, lb, hgrn_norm_g[0])

    hres = _proj2_residual(oa.reshape(m, w_attn), ob.reshape(m, w_attn),
                           cast(ab_w_out[0]), hres)
    hres = _mlp(hres, mlp_norm_g[0], cast(mlp_w_up[0]), cast(mlp_w_down[0]),
                final_norm_g, False)

    bgate, u = _conv_inproj(hres, mix_norm_g[1], conv_w_in[0])
    hres = _conv_mixer(bgate, u, conv_w[0], cast(conv_w_out[0]), hres, seq)
    hres = _mlp(hres, mlp_norm_g[1], cast(mlp_w_up[1]), cast(mlp_w_down[1]),
                final_norm_g, True)
    return hres.reshape(bsz, seq, d)
```
